```python
import jax, jax.numpy as jnp
from jax import lax
import numpy as np

D_MODEL = 4096
BATCH = 2
SEQ = 8192
DEPTH = 4

HEAD_DIM = 128
ROPE_THETA = 500000.0
ROT_FRACTION = 4
Q_BLOCK = 128
NORM_EPS = 1e-6
NEG_INF = -1e30
MAX_POS_OFFSET = 1024

A_HEADS = 12
A_W = A_HEADS * HEAD_DIM
B_HEADS = 8
B_W = B_HEADS * HEAD_DIM
B_Q_LORA = 1024
IDX_HEADS = 32
IDX_DIM = 64
IDX_TOPK_MAX = 256
IDX_SCALE = (IDX_HEADS * IDX_DIM) ** -0.5
C_HEADS = 12
C_W = C_HEADS * HEAD_DIM
C_GROUPS = 2
C_KV_W = C_GROUPS * HEAD_DIM
CMP_BLOCK = 32
CMP_STRIDE = 16
SLC_BLOCK = 64
SLC_TOPK = 16
WINDOW = 512
FORCE_BONUS = 1e4
MIX_WIDTH = A_W + B_W + C_W
GATE_RANK = 512
D_FF = -(-8 * D_MODEL // (3 * 256)) * 256

IN_WIDTHS = (A_W, A_W, A_W, A_HEADS,
             B_Q_LORA, HEAD_DIM, HEAD_DIM, IDX_DIM, IDX_HEADS,
             C_W, C_KV_W, C_KV_W, C_KV_W, C_KV_W, C_KV_W, C_KV_W, 3 * C_HEADS,
             GATE_RANK)
IN_WIDTH = 3 * A_W + A_HEADS + B_Q_LORA + 2 * HEAD_DIM + IDX_DIM + IDX_HEADS + C_W + 6 * C_KV_W + 3 * C_HEADS + GATE_RANK

kernel_name = 'hybrid_fox_dsa_nsa_gated_block'


def _rms_norm(x, g):
    xf = x.astype(jnp.float32)
    y = xf * lax.rsqrt(jnp.mean(xf * xf, axis=-1, keepdims=True) + NORM_EPS)
    return (y * g.astype(jnp.float32)).astype(x.dtype)


def _rotary(x, pos):
    d = x.shape[-1]
    rot = d // ROT_FRACTION
    half = rot // 2
    inv_freq = ROPE_THETA ** (-jnp.arange(half, dtype=jnp.float32) / half)
    ang = pos.astype(jnp.float32)[:, :, None, None] * inv_freq
    cos, sin = jnp.cos(ang), jnp.sin(ang)
    x1 = x[..., :half].astype(jnp.float32)
    x2 = x[..., half:rot].astype(jnp.float32)
    return jnp.concatenate([(x1 * cos - x2 * sin).astype(x.dtype),
                            (x2 * cos + x1 * sin).astype(x.dtype),
                            x[..., rot:]], axis=-1)


def _masked_softmax(logits, mask):
    l = jnp.where(mask, logits, NEG_INF)
    m = jnp.max(l, axis=-1, keepdims=True)
    p = jnp.where(mask, jnp.exp(l - m), 0.0)
    return p / jnp.maximum(jnp.sum(p, axis=-1, keepdims=True), 1e-30)


def _rows(a, t0):
    return lax.dynamic_slice_in_dim(a, t0, Q_BLOCK, axis=1)


def _sweep(block_fn, n_tokens):
    ys = lax.map(block_fn, jnp.arange(n_tokens // Q_BLOCK, dtype=jnp.int32) * Q_BLOCK)
    ys = jnp.moveaxis(ys, 0, 1)
    return ys.reshape(ys.shape[0], n_tokens, ys.shape[-1])


def _fox_attention(q, k, v, log_f):
    B, T, H, Dh = q.shape
    scale = Dh ** -0.5
    F = jnp.cumsum(log_f, axis=1)
    Fk = jnp.swapaxes(F, 1, 2)
    kpos = jnp.arange(T)

    def block(t0):
        qpos = t0 + jnp.arange(Q_BLOCK)
        Fq = jnp.swapaxes(_rows(F, t0), 1, 2)
        s = jnp.einsum('bqhd,bkhd->bhqk', _rows(q, t0), k).astype(jnp.float32) * scale
        s = s + Fq[..., :, None] - Fk[:, :, None, :]
        p = _masked_softmax(s, kpos[None, :] <= qpos[:, None])
        o = jnp.einsum('bhqk,bkhd->bqhd', p.astype(v.dtype), v)
        return o.reshape(B, Q_BLOCK, H * Dh)

    return _sweep(block, T)


def _dsa_attention(q, k, v, iq, ik, iw, topk):
    B, T, H, Dh = q.shape
    scale = Dh ** -0.5
    kpos = jnp.arange(T)
    take = jax.vmap(lambda a, i: a[i])

    def block(t0):
        qpos = t0 + jnp.arange(Q_BLOCK)
        rel = jax.nn.relu(jnp.einsum('bqhd,bkd->bqhk', _rows(iq, t0), ik).astype(jnp.float32))
        score = jnp.einsum('bqhk,bqh->bqk', rel, _rows(iw, t0).astype(jnp.float32))
        score = jnp.where(kpos[None, :] <= qpos[:, None], score, NEG_INF)
        _, idx = lax.top_k(score, topk)
        valid = idx <= qpos[None, :, None]
        kg = take(k, idx)
        vg = take(v, idx)
        s = jnp.einsum('bqhd,bqkd->bhqk', _rows(q, t0), kg).astype(jnp.float32) * scale
        p = _masked_softmax(s, valid[:, None])
        o = jnp.einsum('bhqk,bqkd->bqhd', p.astype(vg.dtype), vg)
        return o.reshape(B, Q_BLOCK, H * Dh)

    return _sweep(block, T)


def _nsa_attention(q, k_cmp, v_cmp, k_slc, v_slc, k_win, v_win, gates,
                   pos_k, pos_v, w_ck, w_cv):
    B, T, H, Dh = q.shape
    G = k_cmp.shape[2]
    n = H // G
    scale = Dh ** -0.5
    n_cmp = (T - CMP_BLOCK) // CMP_STRIDE + 1
    tok = jnp.arange(n_cmp)[:, None] * CMP_STRIDE + jnp.arange(CMP_BLOCK)[None, :]

    def compress(a, pe, w):
        ab = a[:, tok] + pe[:, None, :]
        ab = jnp.swapaxes(ab, 2, 3).reshape(B, n_cmp, G, CMP_BLOCK * Dh)
        return ab @ w

    kc = compress(k_cmp, pos_k, w_ck)
    vc = compress(v_cmp, pos_v, w_cv)
    cend = jnp.arange(n_cmp) * CMP_STRIDE + CMP_BLOCK - 1
    n_slc = T // SLC_BLOCK
    ksel = min(SLC_TOPK, n_slc)
    per_blk = SLC_BLOCK // CMP_STRIDE
    pad = n_slc * per_blk - n_cmp
    ks_blocks = jnp.transpose(k_slc.reshape(B, n_slc, SLC_BLOCK, G, Dh), (0, 3, 1, 2, 4))
    vs_blocks = jnp.transpose(v_slc.reshape(B, n_slc, SLC_BLOCK, G, Dh), (0, 3, 1, 2, 4))
    take = jax.vmap(jax.vmap(lambda a, i: a[i]))
    blk = jnp.arange(n_slc)
    blk_start = blk * SLC_BLOCK
    kw = jnp.pad(k_win, ((0, 0), (WINDOW, 0), (0, 0), (0, 0)))
    vw = jnp.pad(v_win, ((0, 0), (WINDOW, 0), (0, 0), (0, 0)))

    def block(t0):
        qpos = t0 + jnp.arange(Q_BLOCK)
        qb = _rows(q, t0).reshape(B, Q_BLOCK, G, n, Dh)
        s_c = jnp.einsum('bqgnd,bcgd->bgnqc', qb, kc).astype(jnp.float32) * scale
        p_c = _masked_softmax(s_c, cend[None, :] <= qpos[:, None])
        o_c = jnp.einsum('bgnqc,bcgd->bqgnd', p_c.astype(vc.dtype), vc)
        imp = jnp.pad(jnp.sum(p_c, axis=2), ((0, 0), (0, 0), (0, 0), (0, pad)))
        imp = jnp.sum(imp.reshape(B, G, Q_BLOCK, n_slc, per_blk), axis=-1)
        cur = qpos // SLC_BLOCK
        forced = (blk[None, :] == 0) | (blk[None, :] == cur[:, None]) | (blk[None, :] == cur[:, None] - 1)
        imp = jnp.where(blk_start[None, :] <= qpos[:, None],
                        imp + jnp.where(forced, FORCE_BONUS, 0.0), NEG_INF)
        _, bidx = lax.top_k(imp, ksel)
        L = ksel * SLC_BLOCK
        kg = take(ks_blocks, bidx).reshape(B, G, Q_BLOCK, L, Dh)
        vg = take(vs_blocks, bidx).reshape(B, G, Q_BLOCK, L, Dh)
        tpos = (bidx[..., None] * SLC_BLOCK + jnp.arange(SLC_BLOCK)).reshape(B, G, Q_BLOCK, L)
        s_s = jnp.einsum('bqgnd,bgqld->bgnql', qb, kg).astype(jnp.float32) * scale
        p_s = _masked_softmax(s_s, (tpos <= qpos[:, None])[:, :, None])
        o_s = jnp.einsum('bgnql,bgqld->bqgnd', p_s.astype(vg.dtype), vg)
        kwb = lax.dynamic_slice_in_dim(kw, t0, WINDOW + Q_BLOCK, axis=1)
        vwb = lax.dynamic_slice_in_dim(vw, t0, WINDOW + Q_BLOCK, axis=1)
        wpos = t0 - WINDOW + jnp.arange(WINDOW + Q_BLOCK)
        dist = qpos[:, None] - wpos[None, :]
        wmask = (wpos[None, :] >= 0) & (dist >= 0) & (dist < WINDOW)
        s_w = jnp.einsum('bqgnd,bkgd->bgnqk', qb, kwb).astype(jnp.float32) * scale
        p_w = _masked_softmax(s_w, wmask)
        o_w = jnp.einsum('bgnqk,bkgd->bqgnd', p_w.astype(vwb.dtype), vwb)
        g = _rows(gates, t0).reshape(B, Q_BLOCK, G, n, 3)
        o = g[..., 0:1] * o_c + g[..., 1:2] * o_s + g[..., 2:3] * o_w
        return o.reshape(B, Q_BLOCK, H * Dh)

    return _sweep(block, T)


def _hybrid_mixer(h, positions, w_in, fox_forget_bias, dsa_q_norm, w_dsa_uq, w_dsa_iq,
                  nsa_cmp_pos_k, nsa_cmp_pos_v, w_nsa_cmp_k, w_nsa_cmp_v, nsa_gate_bias,
                  w_gate_up, b_gate, w_branch, w_out, topk_b):
    B, T, _ = h.shape
    offsets = [int(o) for o in np.cumsum(IN_WIDTHS)[:-1]]
    (a_q, a_k, a_v, a_f, b_cq, b_k, b_v, b_ik, b_iw,
     c_q, c_kc, c_vc, c_ks, c_vs, c_kw, c_vw, c_g, z) = jnp.split(h @ w_in, offsets, axis=-1)
    log_f = jax.nn.log_sigmoid((a_f + fox_forget_bias).astype(jnp.float32))
    o_a = _fox_attention(a_q.reshape(B, T, A_HEADS, HEAD_DIM), a_k.reshape(B, T, A_HEADS, HEAD_DIM),
                         a_v.reshape(B, T, A_HEADS, HEAD_DIM), log_f)
    cqn = _rms_norm(b_cq, dsa_q_norm)
    q_b = _rotary((cqn @ w_dsa_uq).reshape(B, T, B_HEADS, HEAD_DIM), positions)
    k_b = _rotary(b_k[:, :, None, :], positions)[:, :, 0]
    iq = _rotary((cqn @ w_dsa_iq).reshape(B, T, IDX_HEADS, IDX_DIM), positions)
    ik = _rotary(b_ik[:, :, None, :], positions)[:, :, 0]
    o_b = _dsa_attention(q_b, k_b, b_v, iq, ik, b_iw * IDX_SCALE, topk_b)
    kv_shape = (B, T, C_GROUPS, HEAD_DIM)
    q_c = _rotary(c_q.reshape(B, T, C_HEADS, HEAD_DIM), positions)
    gates_c = jax.nn.sigmoid((c_g + nsa_gate_bias).astype(jnp.float32)).astype(h.dtype).reshape(B, T, C_HEADS, 3)
    o_c = _nsa_attention(q_c, c_kc.reshape(kv_shape), c_vc.reshape(kv_shape),
                         _rotary(c_ks.reshape(kv_shape), positions), c_vs.reshape(kv_shape),
                         _rotary(c_kw.reshape(kv_shape), positions), c_vw.reshape(kv_shape),
                         gates_c, nsa_cmp_pos_k, nsa_cmp_pos_v, w_nsa_cmp_k, w_nsa_cmp_v)
    g = jax.nn.sigmoid((z @ w_gate_up + b_gate).astype(jnp.float32)).astype(h.dtype).reshape(B, T, 3, D_MODEL)
    y = (g[:, :, 0] * (o_a @ w_branch[:A_W])
         + g[:, :, 1] * (o_b @ w_branch[A_W:A_W + B_W])
         + g[:, :, 2] * (o_c @ w_branch[A_W + B_W:]))
    return y @ w_out


def setup_inputs(seed: int = 0) -> dict:
    key = jax.random.key(seed)
    ks = jax.random.split(key, 24)
    f32 = jnp.float32

    def nrm(k, shape, scale):
        return jax.random.normal(k, shape, f32) * scale

    def gain(k, width):
        return 1.0 + 0.02 * jax.random.normal(k, (DEPTH, width), f32)

    x = jax.random.normal(ks[0], (BATCH, SEQ, D_MODEL), f32)
    offset = jax.random.randint(ks[1], (BATCH, 1), 0, MAX_POS_OFFSET, dtype=jnp.int32)
    positions = offset + jnp.arange(SEQ, dtype=jnp.int32)[None, :]
    return {
        'x': x,
        'positions': positions,
        'g_mix_pre': gain(ks[2], D_MODEL),
        'g_mix_post': gain(ks[3], D_MODEL),
        'g_ffn_pre': gain(ks[4], D_MODEL),
        'g_ffn_post': gain(ks[5], D_MODEL),
        'w_in': nrm(ks[6], (DEPTH, D_MODEL, IN_WIDTH), D_MODEL ** -0.5),
        'fox_forget_bias': jax.random.uniform(ks[7], (DEPTH, A_HEADS), f32, 1.0, 6.0),
        'dsa_q_norm': gain(ks[8], B_Q_LORA),
        'w_dsa_uq': nrm(ks[9], (DEPTH, B_Q_LORA, B_W), B_Q_LORA ** -0.5),
        'w_dsa_iq': nrm(ks[10], (DEPTH, B_Q_LORA, IDX_HEADS * IDX_DIM), B_Q_LORA ** -0.5),
        'nsa_cmp_pos_k': nrm(ks[11], (DEPTH, CMP_BLOCK, HEAD_DIM), 0.5),
        'nsa_cmp_pos_v': nrm(ks[12], (DEPTH, CMP_BLOCK, HEAD_DIM), 0.5),
        'w_nsa_cmp_k': nrm(ks[13], (DEPTH, CMP_BLOCK * HEAD_DIM, HEAD_DIM), (CMP_BLOCK * HEAD_DIM) ** -0.5),
        'w_nsa_cmp_v': nrm(ks[14], (DEPTH, CMP_BLOCK * HEAD_DIM, HEAD_DIM), (CMP_BLOCK * HEAD_DIM) ** -0.5),
        'nsa_gate_bias': nrm(ks[15], (DEPTH, 3 * C_HEADS), 0.1),
        'w_gate_up': nrm(ks[16], (DEPTH, GATE_RANK, 3 * D_MODEL), GATE_RANK ** -0.5),
        'b_gate': nrm(ks[17], (DEPTH, 3 * D_MODEL), 0.1),
        'w_branch': nrm(ks[18], (DEPTH, MIX_WIDTH, D_MODEL), A_W ** -0.5),
        'w_out': nrm(ks[19], (DEPTH, D_MODEL, D_MODEL), D_MODEL ** -0.5),
        'w_ffn_gate': nrm(ks[20], (DEPTH, D_MODEL, D_FF), D_MODEL ** -0.5),
        'w_ffn_up': nrm(ks[21], (DEPTH, D_MODEL, D_FF), D_MODEL ** -0.5),
        'w_ffn_down': nrm(ks[22], (DEPTH, D_FF, D_MODEL), D_FF ** -0.5),
    }


def reference(x, positions, g_mix_pre, g_mix_post, g_ffn_pre, g_ffn_post, w_in, fox_forget_bias,
              dsa_q_norm, w_dsa_uq, w_dsa_iq, nsa_cmp_pos_k, nsa_cmp_pos_v, w_nsa_cmp_k, w_nsa_cmp_v,
              nsa_gate_bias, w_gate_up, b_gate, w_branch, w_out, w_ffn_gate, w_ffn_up, w_ffn_down):
    T = x.shape[1]
    topk_b = min(IDX_TOPK_MAX, T // 4)
    for l in range(DEPTH):
        h = _rms_norm(x, g_mix_pre[l])
        m = _hybrid_mixer(h, positions, w_in[l], fox_forget_bias[l], dsa_q_norm[l], w_dsa_uq[l],
                          w_dsa_iq[l], nsa_cmp_pos_k[l], nsa_cmp_pos_v[l], w_nsa_cmp_k[l],
                          w_nsa_cmp_v[l], nsa_gate_bias[l], w_gate_up[l], b_gate[l], w_branch[l],
                          w_out[l], topk_b)
        x = x + _rms_norm(m, g_mix_post[l])
        h = _rms_norm(x, g_ffn_pre[l])
        f = (jax.nn.silu(h @ w_ffn_gate[l]) * (h @ w_ffn_up[l])) @ w_ffn_down[l]
        x = x + _rms_norm(f, g_ffn_post[l])
    return x
```

```python
import functools

import jax
import jax.numpy as jnp
import numpy as np
from jax import lax
from jax.experimental import pallas as pl
from jax.experimental.pallas import tpu as pltpu

F32 = jnp.float32
BF16 = jnp.bfloat16
I32 = jnp.int32

HEAD_DIM = 128
LANES = 128
ROPE_THETA = 500000.0
ROT_FRACTION = 4
NORM_EPS = 1e-6
NEG_INF = -1e30
INT_MIN = -(2 ** 31)

A_HEADS = 12
A_W = A_HEADS * HEAD_DIM
B_HEADS = 8
B_W = B_HEADS * HEAD_DIM
B_Q_LORA = 1024
IDX_HEADS = 32
IDX_DIM = 64
IDX_TOPK_MAX = 256
IDX_SCALE = (IDX_HEADS * IDX_DIM) ** -0.5
C_HEADS = 12
C_W = C_HEADS * HEAD_DIM
C_GROUPS = 2
C_GROUP_HEADS = C_HEADS // C_GROUPS
C_KV_W = C_GROUPS * HEAD_DIM
CMP_BLOCK = 32
CMP_STRIDE = 16
SLC_BLOCK = 64
SLC_TOPK = 16
CMP_PER_SLC = SLC_BLOCK // CMP_STRIDE
WINDOW = 512
FORCE_BONUS = 1e4
GATE_RANK = 512
Q_BLOCK = 128

IN_WIDTHS = (A_W, A_W, A_W, A_HEADS,
             B_Q_LORA, HEAD_DIM, HEAD_DIM, IDX_DIM, IDX_HEADS,
             C_W, C_KV_W, C_KV_W, C_KV_W, C_KV_W, C_KV_W, C_KV_W, 3 * C_HEADS,
             GATE_RANK)
IN_NAMES = ("a_q", "a_k", "a_v", "a_f", "b_cq", "b_k", "b_v", "b_ik", "b_iw",
            "c_q", "c_kc", "c_vc", "c_ks", "c_vs", "c_kw", "c_vw", "c_g", "z")

PB = dict(b_cq=0, z=8, a_q=12, a_k=24, a_v=36, b_k=48, b_v=49, c_q=50,
          c_kc=62, c_vc=64, c_ks=66, c_vs=68, c_kw=70, c_vw=72,
          a_f=74, ik_lo=75, ik_hi=76, b_iw=77, c_g=78)
P_BLOCKS = 80
RB = dict(c_q=0, b_k=12, c_ks=13, c_kw=15, ik_lo=17, ik_hi=18)

VMEM_LIMIT = 56 * 1024 * 1024


def _cparams(sem, vmem=VMEM_LIMIT):
    return pltpu.CompilerParams(dimension_semantics=sem, vmem_limit_bytes=vmem)


def _dot(a, b):
    return jnp.dot(a, b, preferred_element_type=F32)


def _dot_tb(a, b):
    return lax.dot_general(a, b, (((1,), (1,)), ((), ())), preferred_element_type=F32)


def _dot_ta(a, b):
    return lax.dot_general(a, b, (((0,), (0,)), ((), ())), preferred_element_type=F32)


def _rmsnorm_kernel(x_ref, g_ref, o_ref):
    x = x_ref[...].astype(F32)
    y = x * lax.rsqrt(jnp.mean(x * x, axis=-1, keepdims=True) + NORM_EPS)
    o_ref[...] = (y * g_ref[...]).astype(o_ref.dtype)


def rmsnorm(x, g, *, width, col_block=0, tm=512):
    M = x.shape[0]
    tm = min(tm, M)
    return pl.pallas_call(
        _rmsnorm_kernel,
        grid=(M // tm,),
        in_specs=[pl.BlockSpec((tm, width), lambda i: (i, col_block)),
                  pl.BlockSpec((1, width), lambda i: (0, 0))],
        out_specs=pl.BlockSpec((tm, width), lambda i: (i, 0)),
        out_shape=jax.ShapeDtypeStruct((M, width), BF16),
        compiler_params=_cparams(("parallel",)),
        name="rmsnorm",
    )(x, g.reshape(1, width).astype(F32))


def _resnorm_kernel(x_ref, m_ref, g1_ref, g2_ref, xo_ref, ho_ref):
    m = m_ref[...]
    y = m * lax.rsqrt(jnp.mean(m * m, axis=-1, keepdims=True) + NORM_EPS) * g1_ref[...]
    xn = x_ref[...] + y
    xo_ref[...] = xn
    h = xn * lax.rsqrt(jnp.mean(xn * xn, axis=-1, keepdims=True) + NORM_EPS) * g2_ref[...]
    ho_ref[...] = h.astype(ho_ref.dtype)


def _resnorm_last_kernel(x_ref, m_ref, g1_ref, xo_ref):
    m = m_ref[...]
    y = m * lax.rsqrt(jnp.mean(m * m, axis=-1, keepdims=True) + NORM_EPS) * g1_ref[...]
    xo_ref[...] = x_ref[...] + y


def resnorm(x, m, g1, g2=None, *, tm=256):
    M, D = x.shape
    tm = min(tm, M)
    row = pl.BlockSpec((tm, D), lambda i: (i, 0))
    gain = pl.BlockSpec((1, D), lambda i: (0, 0))
    if g2 is None:
        return pl.pallas_call(
            _resnorm_last_kernel, grid=(M // tm,),
            in_specs=[row, row, gain], out_specs=row,
            out_shape=jax.ShapeDtypeStruct((M, D), F32),
            compiler_params=_cparams(("parallel",)), name="resnorm_last",
        )(x, m, g1.reshape(1, D))
    return pl.pallas_call(
        _resnorm_kernel, grid=(M // tm,),
        in_specs=[row, row, gain, gain], out_specs=[row, row],
        out_shape=[jax.ShapeDtypeStruct((M, D), F32), jax.ShapeDtypeStruct((M, D), BF16)],
        compiler_params=_cparams(("parallel",)), name="resnorm",
    )(x, m, g1.reshape(1, D), g2.reshape(1, D))


def _mm_kernel(x_ref, w_ref, *o_refs):
    acc = _dot(x_ref[...], w_ref[...])
    for o_ref in o_refs:
        o_ref[...] = acc.astype(o_ref.dtype)


def matmul(x, w, out_dtypes, *, tm=1024, tn=512):
    M, K = x.shape
    N = w.shape[1]
    tm, tn = min(tm, M), min(tn, N)
    outs = pl.pallas_call(
        _mm_kernel,
        grid=(M // tm, N // tn),
        in_specs=[pl.BlockSpec((tm, K), lambda i, j: (i, 0)),
                  pl.BlockSpec((K, tn), lambda i, j: (0, j))],
        out_specs=[pl.BlockSpec((tm, tn), lambda i, j: (i, j)) for _ in out_dtypes],
        out_shape=[jax.ShapeDtypeStruct((M, N), dt) for dt in out_dtypes],
        compiler_params=_cparams(("parallel", "arbitrary")),
        name="matmul",
    )(x, w)
    return outs


def _mm_ksplit_kernel(x_ref, w_ref, o_ref, acc_ref):
    k = pl.program_id(2)

    @pl.when(k == 0)
    def _():
        acc_ref[...] = jnp.zeros_like(acc_ref)

    acc_ref[...] += _dot(x_ref[...], w_ref[...])

    @pl.when(k == pl.num_programs(2) - 1)
    def _():
        o_ref[...] = acc_ref[...].astype(o_ref.dtype)


def matmul_ksplit(x, w, out_dtype, *, tm, tn, tk):
    M, K = x.shape
    N = w.shape[1]
    tm, tn, tk = min(tm, M), min(tn, N), min(tk, K)
    return pl.pallas_call(
        _mm_ksplit_kernel,
        grid=(M // tm, N // tn, K // tk),
        in_specs=[pl.BlockSpec((tm, tk), lambda i, j, k: (i, k)),
                  pl.BlockSpec((tk, tn), lambda i, j, k: (k, j))],
        out_specs=pl.BlockSpec((tm, tn), lambda i, j, k: (i, j)),
        out_shape=jax.ShapeDtypeStruct((M, N), out_dtype),
        scratch_shapes=[pltpu.VMEM((tm, tn), F32)],
        compiler_params=_cparams(("parallel", "arbitrary", "arbitrary")),
        name="matmul_ksplit",
    )(x, w)


def _ffn_up_kernel(h_ref, wg_ref, wu_ref, o_ref):
    h = h_ref[...]
    g = _dot(h, wg_ref[...])
    u = _dot(h, wu_ref[...])
    o_ref[...] = (g * jax.nn.sigmoid(g) * u).astype(o_ref.dtype)


def ffn_up(h, wg, wu, *, tm=1024, tn=256):
    M, K = h.shape
    N = wg.shape[1]
    tm, tn = min(tm, M), min(tn, N)
    wspec = pl.BlockSpec((K, tn), lambda i, j: (0, j))
    return pl.pallas_call(
        _ffn_up_kernel,
        grid=(M // tm, N // tn),
        in_specs=[pl.BlockSpec((tm, K), lambda i, j: (i, 0)), wspec, wspec],
        out_specs=pl.BlockSpec((tm, tn), lambda i, j: (i, j)),
        out_shape=jax.ShapeDtypeStruct((M, N), BF16),
        compiler_params=_cparams(("parallel", "arbitrary")),
        name="ffn_up",
    )(h, wg, wu)


def _merge_kernel(z_ref, oa_ref, ob_ref, oc_ref, wga_ref, wgb_ref, wgc_ref,
                  bga_ref, bgb_ref, bgc_ref, wa_ref, wb_ref, wc_ref, y_ref):
    z = z_ref[...]

    def branch(o_ref, w_ref, wg_ref, bg_ref):
        gate = jax.nn.sigmoid(_dot(z, wg_ref[...]) + bg_ref[...])
        return gate * _dot(o_ref[...], w_ref[...])

    y = (branch(oa_ref, wa_ref, wga_ref, bga_ref)
         + branch(ob_ref, wb_ref, wgb_ref, bgb_ref)
         + branch(oc_ref, wc_ref, wgc_ref, bgc_ref))
    y_ref[...] = y.astype(y_ref.dtype)


def merge(projb, o_a, o_b, o_c, w_gate_up, b_gate, wa, wb, wc, *, tm=1024, tn=512):
    M = projb.shape[0]
    D = wa.shape[1]
    tm, tn = min(tm, M), min(tn, D)
    nj = D // tn
    z_blk = PB["z"] * LANES // GATE_RANK

    def full_k(o):
        return pl.BlockSpec((tm, o.shape[1]), lambda i, j: (i, 0))

    def gate_w(b):
        return pl.BlockSpec((GATE_RANK, tn), lambda i, j, b=b: (0, b * nj + j))

    def gate_b(b):
        return pl.BlockSpec((1, tn), lambda i, j, b=b: (0, b * nj + j))

    def branch_w(w):
        return pl.BlockSpec((w.shape[0], tn), lambda i, j: (0, j))

    return pl.pallas_call(
        _merge_kernel,
        grid=(M // tm, nj),
        in_specs=[pl.BlockSpec((tm, GATE_RANK), lambda i, j: (i, z_blk)),
                  full_k(o_a), full_k(o_b), full_k(o_c),
                  gate_w(0), gate_w(1), gate_w(2), gate_b(0), gate_b(1), gate_b(2),
                  branch_w(wa), branch_w(wb), branch_w(wc)],
        out_specs=pl.BlockSpec((tm, tn), lambda i, j: (i, j)),
        out_shape=jax.ShapeDtypeStruct((M, D), BF16),
        compiler_params=_cparams(("parallel", "arbitrary")),
        name="merge",
    )(projb, o_a, o_b, o_c, w_gate_up, w_gate_up, w_gate_up, b_gate, b_gate, b_gate, wa, wb, wc)


def _rope_kernel(blk_ref, x_ref, c_ref, s1_ref, s2_ref, o_ref, *, shift):
    del blk_ref
    x = x_ref[...]
    up = pltpu.roll(x, LANES - shift, axis=1)
    dn = pltpu.roll(x, shift, axis=1)
    o_ref[...] = (x * c_ref[...] + up * s1_ref[...] + dn * s2_ref[...]).astype(o_ref.dtype)


def rope(x, blocks, tables, *, shift, tm=1024):
    M = x.shape[0]
    tm = min(tm, M)
    nb = len(blocks)
    tab = pl.BlockSpec((tm, LANES), lambda i, j, blk: (i, 0))
    return pl.pallas_call(
        functools.partial(_rope_kernel, shift=shift),
        grid_spec=pltpu.PrefetchScalarGridSpec(
            num_scalar_prefetch=1,
            grid=(M // tm, nb),
            in_specs=[pl.BlockSpec((tm, LANES), lambda i, j, blk: (i, blk[j])), tab, tab, tab],
            out_specs=pl.BlockSpec((tm, LANES), lambda i, j, blk: (i, j)),
        ),
        out_shape=jax.ShapeDtypeStruct((M, nb * LANES), BF16),
        compiler_params=_cparams(("parallel", "arbitrary")),
        name="rope",
    )(jnp.asarray(blocks, I32), x, *tables)


def rope_tables(positions, head_dim):
    B, T = positions.shape
    rot = head_dim // ROT_FRACTION
    half = rot // 2
    inv_freq = ROPE_THETA ** (-jnp.arange(half, dtype=F32) / half)
    ang = positions.astype(F32)[:, :, None] * inv_freq
    cos, sin = jnp.cos(ang), jnp.sin(ang)
    rest = head_dim - rot
    zh = jnp.zeros((B, T, half), F32)
    c = jnp.concatenate([cos, cos, jnp.ones((B, T, rest), F32)], axis=-1)
    s1 = jnp.concatenate([-sin, zh, jnp.zeros((B, T, rest), F32)], axis=-1)
    s2 = jnp.concatenate([zh, sin, jnp.zeros((B, T, rest), F32)], axis=-1)
    reps = LANES // head_dim
    return tuple(jnp.tile(t, (1, 1, reps)).reshape(B * T, LANES) for t in (c, s1, s2)), half


def _fox_gate_kernel(a_ref, b_ref, f_ref, ft_ref, carry_ref, *, tb):
    @pl.when(pl.program_id(1) == 0)
    def _():
        carry_ref[...] = jnp.zeros_like(carry_ref)

    x = a_ref[0] + b_ref[...]
    log_f = jnp.minimum(x, 0.0) - jnp.log1p(jnp.exp(-jnp.abs(x)))
    row = lax.broadcasted_iota(I32, (tb, tb), 0)
    col = lax.broadcasted_iota(I32, (tb, tb), 1)
    tri = (col <= row).astype(F32)
    cum = jnp.dot(tri, log_f, preferred_element_type=F32,
                  precision=lax.Precision.HIGHEST) + carry_ref[...]
    f_ref[0] = cum
    ft_ref[0] = cum.T[:16, :]
    carry_ref[...] = cum[tb - 1:tb, :]


def fox_gate(proj3, bias_row, *, tb=256):
    B, T, _ = proj3.shape
    tb = min(tb, T)
    return pl.pallas_call(
        functools.partial(_fox_gate_kernel, tb=tb),
        grid=(B, T // tb),
        in_specs=[pl.BlockSpec((1, tb, LANES), lambda b, j: (b, j, PB["a_f"])),
                  pl.BlockSpec((1, LANES), lambda b, j: (0, 0))],
        out_specs=[pl.BlockSpec((1, tb, LANES), lambda b, j: (b, j, 0)),
                   pl.BlockSpec((1, 16, tb), lambda b, j: (b, 0, j))],
        out_shape=[jax.ShapeDtypeStruct((B, T, LANES), F32),
                   jax.ShapeDtypeStruct((B, 16, T), F32)],
        scratch_shapes=[pltpu.VMEM((1, LANES), F32)],
        compiler_params=_cparams(("parallel", "arbitrary")),
        name="fox_gate",
    )(proj3, bias_row)


def _fox_kernel(q_ref, k_ref, v_ref, f_ref, ft_ref, o_ref, *, tq, tk, scale):
    h = pl.program_id(1)
    i = pl.program_id(2)
    q = q_ref[0]
    lane = lax.broadcasted_iota(I32, (tq, LANES), 1)
    fq = jnp.sum(jnp.where(lane == h, f_ref[0], 0.0), axis=1, keepdims=True)
    qpos = i * tq + lax.broadcasted_iota(I32, (tq, tk), 0)
    kcol = lax.broadcasted_iota(I32, (tq, tk), 1)
    nkv = (i * tq + tq + tk - 1) // tk

    def body(j, carry):
        m, l, acc = carry
        r0 = pl.multiple_of(j * tk, tk)
        k = k_ref[0, pl.ds(r0, tk), :]
        v = v_ref[0, pl.ds(r0, tk), :]
        fk = ft_ref[0, 0, pl.ds(j, 1), :]
        s = _dot_tb(q, k) * scale + fq - fk
        mask = (j * tk + kcol) <= qpos
        s = jnp.where(mask, s, NEG_INF)
        m_new = jnp.maximum(m, jnp.max(s, axis=1, keepdims=True))
        p = jnp.where(mask, jnp.exp(s - m_new), 0.0)
        alpha = jnp.exp(m - m_new)
        l = alpha * l + jnp.sum(p, axis=1, keepdims=True)
        acc = alpha * acc + _dot(p.astype(BF16), v)
        return m_new, l, acc

    m0 = jnp.full((tq, 1), NEG_INF, F32)
    l0 = jnp.zeros((tq, 1), F32)
    acc0 = jnp.zeros((tq, HEAD_DIM), F32)
    _, l, acc = lax.fori_loop(0, nkv, body, (m0, l0, acc0))
    o_ref[0] = (acc / jnp.maximum(l, 1e-30)).astype(o_ref.dtype)


def fox_attention(projb3, f, ft, *, tq=256, tk=512):
    B, T, _ = projb3.shape
    tq, tk = min(tq, T), min(tk, T)
    ft4 = ft.reshape(B, 16, T // tk, tk)
    return pl.pallas_call(
        functools.partial(_fox_kernel, tq=tq, tk=tk, scale=HEAD_DIM ** -0.5),
        grid=(B, A_HEADS, T // tq),
        in_specs=[pl.BlockSpec((1, tq, HEAD_DIM), lambda b, h, i: (b, i, PB["a_q"] + h)),
                  pl.BlockSpec((1, T, HEAD_DIM), lambda b, h, i: (b, 0, PB["a_k"] + h)),
                  pl.BlockSpec((1, T, HEAD_DIM), lambda b, h, i: (b, 0, PB["a_v"] + h)),
                  pl.BlockSpec((1, tq, LANES), lambda b, h, i: (b, i, 0)),
                  pl.BlockSpec((1, 1, T // tk, tk), lambda b, h, i: (b, h, 0, 0))],
        out_specs=pl.BlockSpec((1, tq, HEAD_DIM), lambda b, h, i: (b, i, h)),
        out_shape=jax.ShapeDtypeStruct((B, T, A_W), BF16),
        compiler_params=_cparams(("parallel", "parallel", "arbitrary")),
        name="fox_attention",
    )(projb3, projb3, projb3, f, ft4)


def _sortable_key(x):
    b = pltpu.bitcast(x, I32)
    return jnp.where(b >= 0, b, b ^ jnp.int32(0x7FFFFFFF))


def _count_ge(key_ref, ntiles, cand, rows):
    def tile(j, cnt):
        ge = jnp.where(key_ref[j] >= cand, 1, 0).astype(I32)
        return cnt + jnp.sum(ge.reshape(rows // 8, 8, LANES), axis=0)

    cnt = lax.fori_loop(0, ntiles, tile, jnp.zeros((8, LANES), I32))
    return jnp.sum(cnt, axis=0, keepdims=True)


def _kth_largest(key_ref, ntiles, k, rows):
    nonneg = _count_ge(key_ref, ntiles, jnp.zeros((1, LANES), I32), rows)
    base0 = jnp.where(nonneg >= k, 0, INT_MIN).astype(I32)

    def step(s, base):
        cand = base | lax.shift_left(jnp.int32(1), 30 - s)
        cnt = _count_ge(key_ref, ntiles, cand, rows)
        return jnp.where(cnt >= k, cand, base)

    return lax.fori_loop(0, 31, step, base0)


def _topk_need(key_ref, ntiles, thr, k, rows):
    gt = _count_ge(key_ref, ntiles, thr + 1, rows)
    gt = jnp.where(thr == jnp.int32(2 ** 31 - 1), 0, gt)
    return (k - gt).astype(F32)


def _topk_select(key, thr, need, eq_before, slt):
    eq = key == thr
    eq_f = jnp.where(eq, 1.0, 0.0)
    rank = _dot(slt, eq_f.astype(BF16)) + eq_before
    sel = (key > thr) | (eq & (rank < need))
    return sel, eq_before + jnp.sum(eq_f, axis=0, keepdims=True)


def strict_lower(n):
    r = jnp.arange(n)
    return (r[None, :] < r[:, None]).astype(BF16)


def _dsa_kernel(iq_ref, iw_ref, q_ref, ika_ref, ikb_ref, k_ref, v_ref, slt_ref, o_ref,
                iqs_ref, qs_ref, wt_ref, key_ref, *, tk, topk, scale):
    i = pl.program_id(1)
    t0 = i * Q_BLOCK
    nkv = (t0 + Q_BLOCK + tk - 1) // tk
    npair = IDX_HEADS // 2
    for p in range(npair):
        iqs_ref[p * Q_BLOCK:(p + 1) * Q_BLOCK, :] = iq_ref[0, :, p * LANES:(p + 1) * LANES]
    for h in range(B_HEADS):
        qs_ref[h * Q_BLOCK:(h + 1) * Q_BLOCK, :] = q_ref[0, :, h * HEAD_DIM:(h + 1) * HEAD_DIM]
    wt_ref[...] = (iw_ref[0] * IDX_SCALE).T
    qpos = t0 + lax.broadcasted_iota(I32, (tk, LANES), 1)
    krow = lax.broadcasted_iota(I32, (tk, LANES), 0)
    chunk = 4

    def score_tile(j, _):
        r0 = pl.multiple_of(j * tk, tk)
        ika = ika_ref[0, pl.ds(r0, tk), :]
        ikb = ikb_ref[0, pl.ds(r0, tk), :]
        acc = jnp.zeros((tk, LANES), F32)
        for c in range(npair // chunk):
            iqc = iqs_ref[c * chunk * Q_BLOCK:(c + 1) * chunk * Q_BLOCK, :]
            ra = _dot_tb(ika, iqc)
            rb = _dot_tb(ikb, iqc)
            for u in range(chunk):
                p = c * chunk + u
                sl = slice(u * LANES, (u + 1) * LANES)
                acc = acc + jnp.maximum(ra[:, sl], 0.0) * wt_ref[2 * p:2 * p + 1, :]
                acc = acc + jnp.maximum(rb[:, sl], 0.0) * wt_ref[2 * p + 1:2 * p + 2, :]
        acc = jnp.where((j * tk + krow) <= qpos, acc, NEG_INF)
        key_ref[j] = _sortable_key(acc)
        return 0

    lax.fori_loop(0, nkv, score_tile, 0)
    thr = _kth_largest(key_ref, nkv, topk, tk)
    need = _topk_need(key_ref, nkv, thr, topk, tk)
    qs = qs_ref[...]
    slt = slt_ref[...]
    width = B_HEADS * Q_BLOCK

    def attn_tile(j, carry):
        m, l, acc, eq_before = carry
        r0 = pl.multiple_of(j * tk, tk)
        sel, eq_before = _topk_select(key_ref[j], thr, need, eq_before, slt)
        sel = sel & ((j * tk + krow) <= qpos)
        selw = jnp.concatenate([sel] * B_HEADS, axis=1)
        s = _dot_tb(k_ref[0, pl.ds(r0, tk), :], qs) * scale
        s = jnp.where(selw, s, NEG_INF)
        m_new = jnp.maximum(m, jnp.max(s, axis=0, keepdims=True))
        p = jnp.where(selw, jnp.exp(s - m_new), 0.0)
        alpha = jnp.exp(m - m_new)
        l = alpha * l + jnp.sum(p, axis=0, keepdims=True)
        acc = alpha * acc + _dot_ta(v_ref[0, pl.ds(r0, tk), :], p.astype(BF16))
        return m_new, l, acc, eq_before

    init = (jnp.full((1, width), NEG_INF, F32), jnp.zeros((1, width), F32),
            jnp.zeros((HEAD_DIM, width), F32), jnp.zeros((1, LANES), F32))
    _, l, acc, _ = lax.fori_loop(0, nkv, attn_tile, init)
    out = acc / jnp.maximum(l, 1e-30)
    for h in range(B_HEADS):
        o_ref[0, :, h * HEAD_DIM:(h + 1) * HEAD_DIM] = (
            out[:, h * Q_BLOCK:(h + 1) * Q_BLOCK].T.astype(o_ref.dtype))


def dsa_attention(rq3, riq3, proj3, rp3, projb3, slt, *, topk, tk=512):
    B, T, _ = proj3.shape
    tk = min(tk, T)
    assert tk >= topk and T % tk == 0
    return pl.pallas_call(
        functools.partial(_dsa_kernel, tk=tk, topk=topk, scale=HEAD_DIM ** -0.5),
        grid=(B, T // Q_BLOCK),
        in_specs=[pl.BlockSpec((1, Q_BLOCK, IDX_HEADS * IDX_DIM), lambda b, i: (b, i, 0)),
                  pl.BlockSpec((1, Q_BLOCK, LANES), lambda b, i: (b, i, PB["b_iw"])),
                  pl.BlockSpec((1, Q_BLOCK, B_W), lambda b, i: (b, i, 0)),
                  pl.BlockSpec((1, T, LANES), lambda b, i: (b, 0, RB["ik_lo"])),
                  pl.BlockSpec((1, T, LANES), lambda b, i: (b, 0, RB["ik_hi"])),
                  pl.BlockSpec((1, T, HEAD_DIM), lambda b, i: (b, 0, RB["b_k"])),
                  pl.BlockSpec((1, T, HEAD_DIM), lambda b, i: (b, 0, PB["b_v"])),
                  pl.BlockSpec((tk, tk), lambda b, i: (0, 0))],
        out_specs=pl.BlockSpec((1, Q_BLOCK, B_W), lambda b, i: (b, i, 0)),
        out_shape=jax.ShapeDtypeStruct((B, T, B_W), BF16),
        scratch_shapes=[pltpu.VMEM((IDX_HEADS // 2 * Q_BLOCK, LANES), BF16),
                        pltpu.VMEM((B_HEADS * Q_BLOCK, HEAD_DIM), BF16),
                        pltpu.VMEM((LANES, Q_BLOCK), F32),
                        pltpu.VMEM((T // tk, tk, LANES), I32)],
        compiler_params=_cparams(("parallel", "arbitrary")),
        name="dsa_attention",
    )(riq3, proj3, rq3, rp3, rp3, rp3, projb3, slt)


def _compress_kernel(x_ref, pe_ref, w_ref, o_ref, *, n_slc):
    def window_sum(r, lo, hi):
        acc = jnp.zeros((n_slc, HEAD_DIM), F32)
        for l in range(lo, hi):
            off = (CMP_STRIDE * r + l) % SLC_BLOCK
            rows = x_ref[0, pl.ds(off, n_slc, stride=SLC_BLOCK), :] + pe_ref[l:l + 1, :]
            acc = acc + _dot(rows.astype(BF16), w_ref[l])
        return acc

    for r in range(CMP_PER_SLC):
        inside = min(CMP_BLOCK, SLC_BLOCK - CMP_STRIDE * r)
        acc = window_sum(r, 0, inside)
        if inside < CMP_BLOCK:
            acc = acc + pltpu.roll(window_sum(r, inside, CMP_BLOCK), n_slc - 1, axis=0)
        o_ref[0, 0, r * n_slc:(r + 1) * n_slc, :] = acc.astype(o_ref.dtype)


def nsa_compress(proj3, col_block, pe, w3):
    B, T, _ = proj3.shape
    n_slc = T // SLC_BLOCK
    return pl.pallas_call(
        functools.partial(_compress_kernel, n_slc=n_slc),
        grid=(B, C_GROUPS),
        in_specs=[pl.BlockSpec((1, T, HEAD_DIM), lambda b, g: (b, 0, col_block + g)),
                  pl.BlockSpec((CMP_BLOCK, HEAD_DIM), lambda b, g: (0, 0)),
                  pl.BlockSpec((CMP_BLOCK, HEAD_DIM, HEAD_DIM), lambda b, g: (0, 0, 0))],
        out_specs=pl.BlockSpec((1, 1, CMP_PER_SLC * n_slc, HEAD_DIM), lambda b, g: (b, g, 0, 0)),
        out_shape=jax.ShapeDtypeStruct((B, C_GROUPS, CMP_PER_SLC * n_slc, HEAD_DIM), BF16),
        compiler_params=_cparams(("parallel", "parallel")),
        name="nsa_compress",
    )(proj3, pe, w3)


def _nsa_kernel(q_ref, kc_ref, vc_ref, ks_ref, vs_ref, kw_ref, vw_ref, g_ref, gb_ref,
                e_ref, slt_ref, o_ref, qs_ref, gt_ref, key_ref, *, tk, n_slc, ksel, scale):
    g = pl.program_id(1)
    i = pl.program_id(2)
    t0 = i * Q_BLOCK
    nh = C_GROUP_HEADS
    width = nh * Q_BLOCK
    for h in range(nh):
        qs_ref[h * Q_BLOCK:(h + 1) * Q_BLOCK, :] = q_ref[0, :, h * HEAD_DIM:(h + 1) * HEAD_DIM]
    qs = qs_ref[...]
    gt_ref[...] = jax.nn.sigmoid(g_ref[0] + gb_ref[...]).T

    def lane_qpos(rows, cols):
        return t0 + lax.broadcasted_iota(I32, (rows, cols), 1) % Q_BLOCK

    def softmax_pv(s, mask, v):
        s = jnp.where(mask, s, NEG_INF)
        m = jnp.max(s, axis=0, keepdims=True)
        p = jnp.where(mask, jnp.exp(s - m), 0.0)
        p = p / jnp.maximum(jnp.sum(p, axis=0, keepdims=True), 1e-30)
        return p, _dot_ta(v, p.astype(BF16))

    nc = CMP_PER_SLC * n_slc
    crow = lax.broadcasted_iota(I32, (nc, width), 0)
    cend = (crow % n_slc) * SLC_BLOCK + (crow // n_slc) * CMP_STRIDE + (CMP_BLOCK - 1)
    s_c = _dot_tb(kc_ref[0, 0], qs) * scale
    p_c, o_cmp = softmax_pv(s_c, cend <= lane_qpos(nc, width), vc_ref[0, 0])

    ph = p_c[:, 0:Q_BLOCK]
    for h in range(1, nh):
        ph = ph + p_c[:, h * Q_BLOCK:(h + 1) * Q_BLOCK]
    imp = ph[0:n_slc, :]
    for r in range(1, CMP_PER_SLC):
        imp = imp + ph[r * n_slc:(r + 1) * n_slc, :]
    blk = lax.broadcasted_iota(I32, (n_slc, Q_BLOCK), 0)
    qp = lane_qpos(n_slc, Q_BLOCK)
    cur = qp // SLC_BLOCK
    forced = (blk == 0) | (blk == cur) | (blk == cur - 1)
    imp = jnp.where(blk * SLC_BLOCK <= qp, imp + jnp.where(forced, FORCE_BONUS, 0.0), NEG_INF)
    key_ref[0] = _sortable_key(imp)
    thr = _kth_largest(key_ref, 1, ksel, n_slc)
    need = _topk_need(key_ref, 1, thr, ksel, n_slc)
    sel, _ = _topk_select(key_ref[0], thr, need, jnp.zeros((1, Q_BLOCK), F32), slt_ref[...])
    sel_b = jnp.where(sel, 1.0, 0.0).astype(BF16)

    nkv = (t0 + Q_BLOCK + tk - 1) // tk
    krow = lax.broadcasted_iota(I32, (tk, Q_BLOCK), 0)
    qp_t = lane_qpos(tk, Q_BLOCK)

    def sel_tile(j, carry):
        m, l, acc = carry
        r0 = pl.multiple_of(j * tk, tk)
        tok = _dot(e_ref[pl.ds(r0, tk), :], sel_b)
        mask = (tok > 0.5) & ((j * tk + krow) <= qp_t)
        maskw = jnp.concatenate([mask] * nh, axis=1)
        s = _dot_tb(ks_ref[0, pl.ds(r0, tk), :], qs) * scale
        s = jnp.where(maskw, s, NEG_INF)
        m_new = jnp.maximum(m, jnp.max(s, axis=0, keepdims=True))
        p = jnp.where(maskw, jnp.exp(s - m_new), 0.0)
        alpha = jnp.exp(m - m_new)
        l = alpha * l + jnp.sum(p, axis=0, keepdims=True)
        acc = alpha * acc + _dot_ta(vs_ref[0, pl.ds(r0, tk), :], p.astype(BF16))
        return m_new, l, acc

    init = (jnp.full((1, width), NEG_INF, F32), jnp.zeros((1, width), F32),
            jnp.zeros((HEAD_DIM, width), F32))
    _, l_s, acc_s = lax.fori_loop(0, nkv, sel_tile, init)
    o_sel = acc_s / jnp.maximum(l_s, 1e-30)

    span = WINDOW + Q_BLOCK
    start = pl.multiple_of(jnp.maximum(t0 - WINDOW, 0), Q_BLOCK)
    wpos = start + lax.broadcasted_iota(I32, (span, width), 0)
    dist = lane_qpos(span, width) - wpos
    s_w = _dot_tb(kw_ref[0, pl.ds(start, span), :], qs) * scale
    _, o_win = softmax_pv(s_w, (dist >= 0) & (dist < WINDOW), vw_ref[0, pl.ds(start, span), :])

    for h in range(nh):
        sl = slice(h * Q_BLOCK, (h + 1) * Q_BLOCK)
        row = 3 * (g * nh + h)
        o = (gt_ref[pl.ds(row, 1), :] * o_cmp[:, sl]
             + gt_ref[pl.ds(row + 1, 1), :] * o_sel[:, sl]
             + gt_ref[pl.ds(row + 2, 1), :] * o_win[:, sl])
        o_ref[0, :, h * HEAD_DIM:(h + 1) * HEAD_DIM] = o.T.astype(o_ref.dtype)


def nsa_attention(rp3, projb3, proj3, kc, vc, gate_bias_row, expand, slt, *, tk=512):
    B, T, _ = proj3.shape
    tk = min(tk, T)
    n_slc = T // SLC_BLOCK
    ksel = min(SLC_TOPK, n_slc)
    gw = C_GROUP_HEADS * HEAD_DIM
    assert T >= WINDOW + Q_BLOCK and T % tk == 0

    def seq(col):
        return pl.BlockSpec((1, T, HEAD_DIM), lambda b, g, i, col=col: (b, 0, col + g))

    cmp_spec = pl.BlockSpec((1, 1, CMP_PER_SLC * n_slc, HEAD_DIM), lambda b, g, i: (b, g, 0, 0))
    return pl.pallas_call(
        functools.partial(_nsa_kernel, tk=tk, n_slc=n_slc, ksel=ksel, scale=HEAD_DIM ** -0.5),
        grid=(B, C_GROUPS, T // Q_BLOCK),
        in_specs=[pl.BlockSpec((1, Q_BLOCK, gw), lambda b, g, i: (b, i, g)),
                  cmp_spec, cmp_spec,
                  seq(RB["c_ks"]), seq(PB["c_vs"]), seq(RB["c_kw"]), seq(PB["c_vw"]),
                  pl.BlockSpec((1, Q_BLOCK, LANES), lambda b, g, i: (b, i, PB["c_g"])),
                  pl.BlockSpec((1, LANES), lambda b, g, i: (0, 0)),
                  pl.BlockSpec((T, n_slc), lambda b, g, i: (0, 0)),
                  pl.BlockSpec((n_slc, n_slc), lambda b, g, i: (0, 0))],
        out_specs=pl.BlockSpec((1, Q_BLOCK, gw), lambda b, g, i: (b, i, g)),
        out_shape=jax.ShapeDtypeStruct((B, T, C_W), BF16),
        scratch_shapes=[pltpu.VMEM((C_GROUP_HEADS * Q_BLOCK, HEAD_DIM), BF16),
                        pltpu.VMEM((LANES, Q_BLOCK), F32),
                        pltpu.VMEM((1, n_slc, Q_BLOCK), I32)],
        compiler_params=_cparams(("parallel", "parallel", "arbitrary")),
        name="nsa_attention",
    )(rp3, kc, vc, rp3, projb3, rp3, projb3, proj3, gate_bias_row, expand, slt)


def _pad_cols(w, width):
    return jnp.pad(w, ((0, 0), (0, width - w.shape[1])))


def _permute_w_in(w_in):
    offs = np.concatenate([[0], np.cumsum(IN_WIDTHS)])
    part = {n: w_in[:, offs[k]:offs[k + 1]] for k, n in enumerate(IN_NAMES)}
    zeros64 = jnp.zeros((w_in.shape[0], IDX_DIM), w_in.dtype)
    order = sorted(PB, key=PB.get)
    cols = []
    for name, nxt in zip(order, order[1:] + [None]):
        end = (PB[nxt] if nxt else P_BLOCKS) * LANES
        if name == "ik_lo":
            blk = jnp.concatenate([part["b_ik"], zeros64], axis=1)
        elif name == "ik_hi":
            blk = jnp.concatenate([zeros64, part["b_ik"]], axis=1)
        else:
            blk = part[name]
        cols.append(_pad_cols(blk, end - PB[name] * LANES))
    return jnp.concatenate(cols, axis=1).astype(BF16)


def _lane_row(v):
    return _pad_cols(v.reshape(1, -1).astype(F32), LANES)


def kernel(x, positions, g_mix_pre, g_mix_post, g_ffn_pre, g_ffn_post, w_in, fox_forget_bias,
           dsa_q_norm, w_dsa_uq, w_dsa_iq, nsa_cmp_pos_k, nsa_cmp_pos_v, w_nsa_cmp_k,
           w_nsa_cmp_v, nsa_gate_bias, w_gate_up, b_gate, w_branch, w_out, w_ffn_gate,
           w_ffn_up, w_ffn_down):
    B, T, D = x.shape
    depth = w_in.shape[0]
    M = B * T
    topk_b = min(IDX_TOPK_MAX, T // 4)
    n_slc = T // SLC_BLOCK
    tk = min(512, T)

    tab128, half128 = rope_tables(positions, HEAD_DIM)
    tab64, half64 = rope_tables(positions, IDX_DIM)
    slt_tk = strict_lower(tk)
    slt_blk = strict_lower(n_slc)
    expand = (jnp.arange(T)[:, None] // SLC_BLOCK == jnp.arange(n_slc)[None, :]).astype(BF16)
    rope_src = ([PB["c_q"] + h for h in range(C_HEADS)] + [PB["b_k"]]
                + [PB["c_ks"], PB["c_ks"] + 1, PB["c_kw"], PB["c_kw"] + 1])
    d_ff = w_ffn_gate.shape[-1]
    ff_pad = -(-d_ff // 512) * 512

    xf = x.reshape(M, D)
    h = rmsnorm(xf, g_mix_pre[0], width=D)
    for l in range(depth):
        proj, projb = matmul(h, _permute_w_in(w_in[l]), (F32, BF16))
        proj3 = proj.reshape(B, T, -1)
        projb3 = projb.reshape(B, T, -1)

        f, ft = fox_gate(proj3, _lane_row(fox_forget_bias[l]))
        o_a = fox_attention(projb3, f, ft)

        cqn = rmsnorm(proj, dsa_q_norm[l], width=B_Q_LORA, col_block=PB["b_cq"] * LANES // B_Q_LORA)
        w_q = jnp.concatenate([w_dsa_uq[l], w_dsa_iq[l]], axis=1).astype(BF16)
        (qiq,) = matmul(cqn, w_q, (F32,))
        rq = rope(qiq, list(range(B_HEADS)), tab128, shift=half128)
        riq = rope(qiq, list(range(B_HEADS, B_HEADS + IDX_HEADS // 2)), tab64, shift=half64)
        rp_main = rope(proj, rope_src, tab128, shift=half128)
        rp_idx = rope(proj, [PB["ik_lo"], PB["ik_hi"]], tab64, shift=half64)
        rp3 = jnp.concatenate([rp_main, rp_idx], axis=1).reshape(B, T, -1)
        o_b = dsa_attention(rq.reshape(B, T, -1), riq.reshape(B, T, -1), proj3, rp3, projb3,
                            slt_tk, topk=topk_b, tk=tk)

        kc = nsa_compress(proj3, PB["c_kc"], nsa_cmp_pos_k[l],
                          w_nsa_cmp_k[l].reshape(CMP_BLOCK, HEAD_DIM, HEAD_DIM).astype(BF16))
        vc = nsa_compress(proj3, PB["c_vc"], nsa_cmp_pos_v[l],
                          w_nsa_cmp_v[l].reshape(CMP_BLOCK, HEAD_DIM, HEAD_DIM).astype(BF16))
        o_c = nsa_attention(rp3, projb3, proj3, kc, vc, _lane_row(nsa_gate_bias[l]),
                            expand, slt_blk, tk=tk)

        wbr = w_branch[l].astype(BF16)
        y = merge(projb, o_a.reshape(M, -1), o_b.reshape(M, -1), o_c.reshape(M, -1),
                  w_gate_up[l].astype(BF16), b_gate[l].reshape(1, -1),
                  wbr[:A_W], wbr[A_W:A_W + B_W], wbr[A_W + B_W:])
        (mix,) = matmul(y, w_out[l].astype(BF16), (F32,))
        xf, h = resnorm(xf, mix, g_mix_post[l], g_ffn_pre[l])

        u = ffn_up(h, _pad_cols(w_ffn_gate[l], ff_pad).astype(BF16),
                   _pad_cols(w_ffn_up[l], ff_pad).astype(BF16), tn=512)
        w_down = jnp.pad(w_ffn_down[l], ((0, ff_pad - d_ff), (0, 0))).astype(BF16)
        ffn = matmul_ksplit(u, w_down, F32, tm=1024, tn=512, tk=ff_pad // 2)
        if l + 1 < depth:
            xf, h = resnorm(xf, ffn, g_ffn_post[l], g_mix_pre[l + 1])
        else:
            xf = resnorm(xf, ffn, g_ffn_post[l])
    return xf.reshape(B, T, D)
```

```python
import functools

import jax
import jax.numpy as jnp
import numpy as np
from jax import lax
from jax.experimental import pallas as pl
from jax.experimental.pallas import tpu as pltpu

F32 = jnp.float32
BF16 = jnp.bfloat16
I32 = jnp.int32

HEAD_DIM = 128
LANES = 128
ROPE_THETA = 500000.0
ROT_FRACTION = 4
NORM_EPS = 1e-6
NEG_INF = -1e30
INT_MIN = -(2 ** 31)

A_HEADS = 12
A_W = A_HEADS * HEAD_DIM
B_HEADS = 8
B_W = B_HEADS * HEAD_DIM
B_Q_LORA = 1024
IDX_HEADS = 32
IDX_DIM = 64
IDX_TOPK_MAX = 256
IDX_SCALE = (IDX_HEADS * IDX_DIM) ** -0.5
C_HEADS = 12
C_W = C_HEADS * HEAD_DIM
C_GROUPS = 2
C_GROUP_HEADS = C_HEADS // C_GROUPS
C_KV_W = C_GROUPS * HEAD_DIM
CMP_BLOCK = 32
CMP_STRIDE = 16
SLC_BLOCK = 64
SLC_TOPK = 16
CMP_PER_SLC = SLC_BLOCK // CMP_STRIDE
WINDOW = 512
FORCE_BONUS = 1e4
GATE_RANK = 512
Q_BLOCK = 128
LOG2E = 1.4426950408889634
QK_FOLD = HEAD_DIM ** -0.5 * LOG2E

IN_WIDTHS = (A_W, A_W, A_W, A_HEADS,
             B_Q_LORA, HEAD_DIM, HEAD_DIM, IDX_DIM, IDX_HEADS,
             C_W, C_KV_W, C_KV_W, C_KV_W, C_KV_W, C_KV_W, C_KV_W, 3 * C_HEADS,
             GATE_RANK)
IN_NAMES = ("a_q", "a_k", "a_v", "a_f", "b_cq", "b_k", "b_v", "b_ik", "b_iw",
            "c_q", "c_kc", "c_vc", "c_ks", "c_vs", "c_kw", "c_vw", "c_g", "z")

PB = dict(b_cq=0, z=8, a_q=12, a_k=24, a_v=36, b_k=48, b_v=49, c_q=50,
          c_kc=62, c_vc=64, c_ks=66, c_vs=68, c_kw=70, c_vw=72,
          a_f=74, ik_lo=75, ik_hi=76, b_iw=77, c_g=78)
P_BLOCKS = 80
RB = dict(c_q=0, b_k=12, c_ks=13, c_kw=15)

VMEM_LIMIT = 56 * 1024 * 1024


def _cparams(sem, vmem=VMEM_LIMIT):
    return pltpu.CompilerParams(dimension_semantics=sem, vmem_limit_bytes=vmem)


def _dot(a, b):
    return jnp.dot(a, b, preferred_element_type=F32)


def _dot_tb(a, b):
    return lax.dot_general(a, b, (((1,), (1,)), ((), ())), preferred_element_type=F32)


def _dot_ta(a, b):
    return lax.dot_general(a, b, (((0,), (0,)), ((), ())), preferred_element_type=F32)


def _rmsnorm_kernel(x_ref, g_ref, o_ref):
    x = x_ref[...].astype(F32)
    y = x * lax.rsqrt(jnp.mean(x * x, axis=-1, keepdims=True) + NORM_EPS)
    o_ref[...] = (y * g_ref[...]).astype(o_ref.dtype)


def rmsnorm(x, g, *, width, col_block=0, tm=512):
    M = x.shape[0]
    tm = min(tm, M)
    return pl.pallas_call(
        _rmsnorm_kernel,
        grid=(M // tm,),
        in_specs=[pl.BlockSpec((tm, width), lambda i: (i, col_block)),
                  pl.BlockSpec((1, width), lambda i: (0, 0))],
        out_specs=pl.BlockSpec((tm, width), lambda i: (i, 0)),
        out_shape=jax.ShapeDtypeStruct((M, width), BF16),
        compiler_params=_cparams(("parallel",)),
        name="rmsnorm",
    )(x, g.reshape(1, width).astype(F32))


def _resnorm_kernel(x_ref, m_ref, g1_ref, g2_ref, xo_ref, ho_ref):
    m = m_ref[...]
    y = m * lax.rsqrt(jnp.mean(m * m, axis=-1, keepdims=True) + NORM_EPS) * g1_ref[...]
    xn = x_ref[...] + y
    xo_ref[...] = xn
    h = xn * lax.rsqrt(jnp.mean(xn * xn, axis=-1, keepdims=True) + NORM_EPS) * g2_ref[...]
    ho_ref[...] = h.astype(ho_ref.dtype)


def _resnorm_last_kernel(x_ref, m_ref, g1_ref, xo_ref):
    m = m_ref[...]
    y = m * lax.rsqrt(jnp.mean(m * m, axis=-1, keepdims=True) + NORM_EPS) * g1_ref[...]
    xo_ref[...] = x_ref[...] + y


def resnorm(x, m, g1, g2=None, *, tm=256):
    M, D = x.shape
    tm = min(tm, M)
    row = pl.BlockSpec((tm, D), lambda i: (i, 0))
    gain = pl.BlockSpec((1, D), lambda i: (0, 0))
    if g2 is None:
        return pl.pallas_call(
            _resnorm_last_kernel, grid=(M // tm,),
            in_specs=[row, row, gain], out_specs=row,
            out_shape=jax.ShapeDtypeStruct((M, D), F32),
            compiler_params=_cparams(("parallel",)), name="resnorm_last",
        )(x, m, g1.reshape(1, D))
    return pl.pallas_call(
        _resnorm_kernel, grid=(M // tm,),
        in_specs=[row, row, gain, gain], out_specs=[row, row],
        out_shape=[jax.ShapeDtypeStruct((M, D), F32), jax.ShapeDtypeStruct((M, D), BF16)],
        compiler_params=_cparams(("parallel",)), name="resnorm",
    )(x, m, g1.reshape(1, D), g2.reshape(1, D))


def _mm_kernel(x_ref, w_ref, *o_refs):
    acc = _dot(x_ref[...], w_ref[...])
    for o_ref in o_refs:
        o_ref[...] = acc.astype(o_ref.dtype)


def matmul(x, w, out_dtypes, *, tm=1024, tn=512):
    M, K = x.shape
    N = w.shape[1]
    tm, tn = min(tm, M), min(tn, N)
    outs = pl.pallas_call(
        _mm_kernel,
        grid=(M // tm, N // tn),
        in_specs=[pl.BlockSpec((tm, K), lambda i, j: (i, 0)),
                  pl.BlockSpec((K, tn), lambda i, j: (0, j))],
        out_specs=[pl.BlockSpec((tm, tn), lambda i, j: (i, j)) for _ in out_dtypes],
        out_shape=[jax.ShapeDtypeStruct((M, N), dt) for dt in out_dtypes],
        compiler_params=_cparams(("parallel", "arbitrary")),
        name="matmul",
    )(x, w)
    return outs


def _mm_ksplit_kernel(x_ref, w_ref, o_ref, acc_ref):
    k = pl.program_id(2)

    @pl.when(k == 0)
    def _():
        acc_ref[...] = jnp.zeros_like(acc_ref)

    acc_ref[...] += _dot(x_ref[...], w_ref[...])

    @pl.when(k == pl.num_programs(2) - 1)
    def _():
        o_ref[...] = acc_ref[...].astype(o_ref.dtype)


def matmul_ksplit(x, w, out_dtype, *, tm, tn, tk):
    M, K = x.shape
    N = w.shape[1]
    tm, tn, tk = min(tm, M), min(tn, N), min(tk, K)
    return pl.pallas_call(
        _mm_ksplit_kernel,
        grid=(M // tm, N // tn, K // tk),
        in_specs=[pl.BlockSpec((tm, tk), lambda i, j, k: (i, k)),
                  pl.BlockSpec((tk, tn), lambda i, j, k: (k, j))],
        out_specs=pl.BlockSpec((tm, tn), lambda i, j, k: (i, j)),
        out_shape=jax.ShapeDtypeStruct((M, N), out_dtype),
        scratch_shapes=[pltpu.VMEM((tm, tn), F32)],
        compiler_params=_cparams(("parallel", "arbitrary", "arbitrary")),
        name="matmul_ksplit",
    )(x, w)


def _ffn_up_kernel(h_ref, wg_ref, wu_ref, o_ref):
    h = h_ref[...]
    g = _dot(h, wg_ref[...])
    u = _dot(h, wu_ref[...])
    o_ref[...] = (g * jax.nn.sigmoid(g) * u).astype(o_ref.dtype)


def ffn_up(h, wg, wu, *, tm=1024, tn=256):
    M, K = h.shape
    N = wg.shape[1]
    tm, tn = min(tm, M), min(tn, N)
    wspec = pl.BlockSpec((K, tn), lambda i, j: (0, j))
    return pl.pallas_call(
        _ffn_up_kernel,
        grid=(M // tm, N // tn),
        in_specs=[pl.BlockSpec((tm, K), lambda i, j: (i, 0)), wspec, wspec],
        out_specs=pl.BlockSpec((tm, tn), lambda i, j: (i, j)),
        out_shape=jax.ShapeDtypeStruct((M, N), BF16),
        compiler_params=_cparams(("parallel", "arbitrary")),
        name="ffn_up",
    )(h, wg, wu)


def _merge_kernel(z_ref, oa_ref, ob_ref, oc_ref, wga_ref, wgb_ref, wgc_ref,
                  bga_ref, bgb_ref, bgc_ref, wa_ref, wb_ref, wc_ref, y_ref):
    z = z_ref[...]

    def branch(o_ref, w_ref, wg_ref, bg_ref):
        gate = jax.nn.sigmoid(_dot(z, wg_ref[...]) + bg_ref[...])
        return gate * _dot(o_ref[...], w_ref[...])

    y = (branch(oa_ref, wa_ref, wga_ref, bga_ref)
         + branch(ob_ref, wb_ref, wgb_ref, bgb_ref)
         + branch(oc_ref, wc_ref, wgc_ref, bgc_ref))
    y_ref[...] = y.astype(y_ref.dtype)


def merge(projb, o_a, o_b, o_c, w_gate_up, b_gate, wa, wb, wc, *, tm=1024, tn=512):
    M = projb.shape[0]
    D = wa.shape[1]
    tm, tn = min(tm, M), min(tn, D)
    nj = D // tn
    z_blk = PB["z"] * LANES // GATE_RANK

    def full_k(o):
        return pl.BlockSpec((tm, o.shape[1]), lambda i, j: (i, 0))

    def gate_w(b):
        return pl.BlockSpec((GATE_RANK, tn), lambda i, j, b=b: (0, b * nj + j))

    def gate_b(b):
        return pl.BlockSpec((1, tn), lambda i, j, b=b: (0, b * nj + j))

    def branch_w(w):
        return pl.BlockSpec((w.shape[0], tn), lambda i, j: (0, j))

    return pl.pallas_call(
        _merge_kernel,
        grid=(M // tm, nj),
        in_specs=[pl.BlockSpec((tm, GATE_RANK), lambda i, j: (i, z_blk)),
                  full_k(o_a), full_k(o_b), full_k(o_c),
                  gate_w(0), gate_w(1), gate_w(2), gate_b(0), gate_b(1), gate_b(2),
                  branch_w(wa), branch_w(wb), branch_w(wc)],
        out_specs=pl.BlockSpec((tm, tn), lambda i, j: (i, j)),
        out_shape=jax.ShapeDtypeStruct((M, D), BF16),
        compiler_params=_cparams(("parallel", "arbitrary")),
        name="merge",
    )(projb, o_a, o_b, o_c, w_gate_up, w_gate_up, w_gate_up, b_gate, b_gate, b_gate, wa, wb, wc)


def _rope_kernel(blk_ref, x_ref, c_ref, s1_ref, s2_ref, o_ref, *, shift):
    del blk_ref
    x = x_ref[...]
    up = pltpu.roll(x, LANES - shift, axis=1)
    dn = pltpu.roll(x, shift, axis=1)
    o_ref[...] = (x * c_ref[...] + up * s1_ref[...] + dn * s2_ref[...]).astype(o_ref.dtype)


def rope(x, blocks, tables, *, shift, tm=1024):
    M = x.shape[0]
    tm = min(tm, M)
    nb = len(blocks)
    tab = pl.BlockSpec((tm, LANES), lambda i, j, blk: (i, 0))
    return pl.pallas_call(
        functools.partial(_rope_kernel, shift=shift),
        grid_spec=pltpu.PrefetchScalarGridSpec(
            num_scalar_prefetch=1,
            grid=(M // tm, nb),
            in_specs=[pl.BlockSpec((tm, LANES), lambda i, j, blk: (i, blk[j])), tab, tab, tab],
            out_specs=pl.BlockSpec((tm, LANES), lambda i, j, blk: (i, j)),
        ),
        out_shape=jax.ShapeDtypeStruct((M, nb * LANES), BF16),
        compiler_params=_cparams(("parallel", "arbitrary")),
        name="rope",
    )(jnp.asarray(blocks, I32), x, *tables)


def rope_tables(positions, head_dim):
    B, T = positions.shape
    rot = head_dim // ROT_FRACTION
    half = rot // 2
    inv_freq = ROPE_THETA ** (-jnp.arange(half, dtype=F32) / half)
    ang = positions.astype(F32)[:, :, None] * inv_freq
    cos, sin = jnp.cos(ang), jnp.sin(ang)
    rest = head_dim - rot
    zh = jnp.zeros((B, T, half), F32)
    c = jnp.concatenate([cos, cos, jnp.ones((B, T, rest), F32)], axis=-1)
    s1 = jnp.concatenate([-sin, zh, jnp.zeros((B, T, rest), F32)], axis=-1)
    s2 = jnp.concatenate([zh, sin, jnp.zeros((B, T, rest), F32)], axis=-1)
    reps = LANES // head_dim
    return tuple(jnp.tile(t, (1, 1, reps)).reshape(B * T, LANES) for t in (c, s1, s2)), half


def _fox_gate_kernel(a_ref, b_ref, f_ref, ft_ref, carry_ref, *, tb):
    @pl.when(pl.program_id(1) == 0)
    def _():
        carry_ref[...] = jnp.zeros_like(carry_ref)

    x = a_ref[0] + b_ref[...]
    log_f = jnp.minimum(x, 0.0) - jnp.log1p(jnp.exp(-jnp.abs(x)))
    row = lax.broadcasted_iota(I32, (tb, tb), 0)
    col = lax.broadcasted_iota(I32, (tb, tb), 1)
    tri = (col <= row).astype(F32)
    cum = jnp.dot(tri, log_f, preferred_element_type=F32,
                  precision=lax.Precision.HIGHEST) + carry_ref[...]
    cum2 = cum * LOG2E
    f_ref[0] = cum2
    ft_ref[0] = cum2.T[:16, :]
    carry_ref[...] = cum[tb - 1:tb, :]


def fox_gate(proj3, bias_row, *, tb=256):
    B, T, _ = proj3.shape
    tb = min(tb, T)
    return pl.pallas_call(
        functools.partial(_fox_gate_kernel, tb=tb),
        grid=(B, T // tb),
        in_specs=[pl.BlockSpec((1, tb, LANES), lambda b, j: (b, j, PB["a_f"])),
                  pl.BlockSpec((1, LANES), lambda b, j: (0, 0))],
        out_specs=[pl.BlockSpec((1, tb, LANES), lambda b, j: (b, j, 0)),
                   pl.BlockSpec((1, 16, tb), lambda b, j: (b, 0, j))],
        out_shape=[jax.ShapeDtypeStruct((B, T, LANES), F32),
                   jax.ShapeDtypeStruct((B, 16, T), F32)],
        scratch_shapes=[pltpu.VMEM((1, LANES), F32)],
        compiler_params=_cparams(("parallel", "arbitrary")),
        name="fox_gate",
    )(proj3, bias_row)


FOX_HEADS_PER_STEP = 1


def _fox_kernel(q_ref, k_ref, v_ref, f_ref, ft_ref, o_ref, *, tq, tk):
    nh = FOX_HEADS_PER_STEP
    hg = pl.program_id(1)
    i = pl.program_id(2)
    lane = lax.broadcasted_iota(I32, (tq, LANES), 1)
    f_rows = f_ref[0]
    heads = [slice(g * HEAD_DIM, (g + 1) * HEAD_DIM) for g in range(nh)]
    q = [q_ref[0, :, sl] for sl in heads]
    fq = [jnp.sum(jnp.where(lane == hg * nh + g, f_rows, 0.0), axis=1, keepdims=True)
          for g in range(nh)]
    nkv = (i * tq + tq + tk - 1) // tk

    def update(j, carry, diagonal):
        r0 = pl.multiple_of(j * tk, tk)
        out = []
        for g in range(nh):
            m, l, acc = carry[g]
            fk = ft_ref[0, g, pl.ds(j, 1), :]
            t = _dot_tb(q[g], k_ref[0, pl.ds(r0, tk), heads[g]]) - fk
            if diagonal:
                qpos = i * tq + lax.broadcasted_iota(I32, (tq, tk), 0)
                kpos = j * tk + lax.broadcasted_iota(I32, (tq, tk), 1)
                t = jnp.where(kpos <= qpos, t, NEG_INF)
            m_new = jnp.maximum(m, jnp.max(t, axis=1, keepdims=True) + fq[g])
            p = jnp.exp2(t + (fq[g] - m_new))
            alpha = jnp.exp2(m - m_new)
            l = alpha * l + jnp.sum(p, axis=1, keepdims=True)
            acc = alpha * acc + _dot(p.astype(BF16), v_ref[0, pl.ds(r0, tk), heads[g]])
            out.append((m_new, l, acc))
        return tuple(out)

    init = tuple((jnp.full((tq, 1), NEG_INF, F32), jnp.zeros((tq, 1), F32),
                  jnp.zeros((tq, HEAD_DIM), F32)) for _ in range(nh))
    carry = lax.fori_loop(0, nkv - 1, lambda j, c: update(j, c, False), init)
    carry = update(nkv - 1, carry, True)
    for g in range(nh):
        _, l, acc = carry[g]
        o_ref[0, :, heads[g]] = (acc / jnp.maximum(l, 1e-30)).astype(o_ref.dtype)


def fox_attention(projb3, f, ft, *, tq=1024, tk=1024):
    B, T, _ = projb3.shape
    tq, tk = min(tq, T), min(tk, T)
    nh = FOX_HEADS_PER_STEP
    gw = nh * HEAD_DIM
    assert tk % tq == 0 and A_HEADS % nh == 0
    assert PB["a_q"] % nh == 0 and PB["a_k"] % nh == 0 and PB["a_v"] % nh == 0
    ft4 = ft.reshape(B, 16, T // tk, tk)
    return pl.pallas_call(
        functools.partial(_fox_kernel, tq=tq, tk=tk),
        grid=(B, A_HEADS // nh, T // tq),
        in_specs=[pl.BlockSpec((1, tq, gw), lambda b, h, i: (b, i, PB["a_q"] // nh + h)),
                  pl.BlockSpec((1, T, gw), lambda b, h, i: (b, 0, PB["a_k"] // nh + h)),
                  pl.BlockSpec((1, T, gw), lambda b, h, i: (b, 0, PB["a_v"] // nh + h)),
                  pl.BlockSpec((1, tq, LANES), lambda b, h, i: (b, i, 0)),
                  pl.BlockSpec((1, nh, T // tk, tk), lambda b, h, i: (b, h, 0, 0))],
        out_specs=pl.BlockSpec((1, tq, gw), lambda b, h, i: (b, i, h)),
        out_shape=jax.ShapeDtypeStruct((B, T, A_W), BF16),
        compiler_params=_cparams(("parallel", "parallel", "arbitrary")),
        name="fox_attention",
    )(projb3, projb3, projb3, f, ft4)


def _sortable_key(x):
    b = pltpu.bitcast(x, I32)
    return jnp.where(b >= 0, b, b ^ jnp.int32(0x7FFFFFFF))


def _count_ge(key_ref, ntiles, cand, rows):
    def tile(j, cnt):
        ge = jnp.where(key_ref[j] >= cand, 1, 0).astype(I32)
        return cnt + jnp.sum(ge.reshape(rows // 8, 8, LANES), axis=0)

    cnt = lax.fori_loop(0, ntiles, tile, jnp.zeros((8, LANES), I32))
    return jnp.sum(cnt, axis=0, keepdims=True)


def _kth_largest(key_ref, ntiles, k, rows):
    nonneg = _count_ge(key_ref, ntiles, jnp.zeros((1, LANES), I32), rows)
    base0 = jnp.where(nonneg >= k, 0, INT_MIN).astype(I32)

    def step(s, base):
        cand = base | lax.shift_left(jnp.int32(1), 30 - s)
        cnt = _count_ge(key_ref, ntiles, cand, rows)
        return jnp.where(cnt >= k, cand, base)

    return lax.fori_loop(0, 31, step, base0)


def _topk_need(key_ref, ntiles, thr, k, rows):
    gt = _count_ge(key_ref, ntiles, thr + 1, rows)
    gt = jnp.where(thr == jnp.int32(2 ** 31 - 1), 0, gt)
    return (k - gt).astype(F32)


def _topk_select(key, thr, need, eq_before, slt):
    eq = key == thr
    eq_f = jnp.where(eq, 1.0, 0.0)
    rank = _dot(slt, eq_f.astype(BF16)) + eq_before
    sel = (key > thr) | (eq & (rank < need))
    return sel, eq_before + jnp.sum(eq_f, axis=0, keepdims=True)


def strict_lower(n):
    r = jnp.arange(n)
    return (r[None, :] < r[:, None]).astype(BF16)


def _softmax_tile_t(s, bias, v, carry, nheads):
    m, l, acc = carry
    ms, ls, ps = [], [], []
    for h in range(nheads):
        sl = slice(h * LANES, (h + 1) * LANES)
        sh = s[:, sl] + bias
        mh = jnp.maximum(m[:, sl], jnp.max(sh, axis=0, keepdims=True))
        ph = jnp.exp2(sh - mh)
        ms.append(mh)
        ls.append(jnp.sum(ph, axis=0, keepdims=True))
        ps.append(ph.astype(BF16))
    m_new = jnp.concatenate(ms, axis=1)
    alpha = jnp.exp2(m - m_new)
    l = alpha * l + jnp.concatenate(ls, axis=1)
    acc = alpha * acc + _dot_ta(v, jnp.concatenate(ps, axis=1))
    return m_new, l, acc


def _softmax_init_t(nheads):
    width = nheads * LANES
    return (jnp.full((1, width), NEG_INF, F32), jnp.zeros((1, width), F32),
            jnp.zeros((HEAD_DIM, width), F32))


def _dsa_kernel(iq_ref, iw_ref, q_ref, ika_ref, ikb_ref, k_ref, v_ref, slt_ref, o_ref,
                iqs_ref, qs_ref, wt_ref, key_ref, *, tk, topk):
    i = pl.program_id(1)
    t0 = i * Q_BLOCK
    nkv = (t0 + Q_BLOCK + tk - 1) // tk
    npair = IDX_HEADS // 2
    for p in range(npair):
        iqs_ref[p * Q_BLOCK:(p + 1) * Q_BLOCK, :] = iq_ref[0, :, p * LANES:(p + 1) * LANES]
    for h in range(B_HEADS):
        qs_ref[h * Q_BLOCK:(h + 1) * Q_BLOCK, :] = q_ref[0, :, h * HEAD_DIM:(h + 1) * HEAD_DIM]
    wt_ref[...] = (iw_ref[0] * IDX_SCALE).T
    qpos = t0 + lax.broadcasted_iota(I32, (tk, LANES), 1)
    krow = lax.broadcasted_iota(I32, (tk, LANES), 0)
    chunk = 4

    def score_tile(j, _):
        r0 = pl.multiple_of(j * tk, tk)
        ika = ika_ref[0, pl.ds(r0, tk), :]
        ikb = ikb_ref[0, pl.ds(r0, tk), :]
        acc = jnp.zeros((tk, LANES), F32)
        for c in range(npair // chunk):
            iqc = iqs_ref[c * chunk * Q_BLOCK:(c + 1) * chunk * Q_BLOCK, :]
            ra = _dot_tb(ika, iqc)
            rb = _dot_tb(ikb, iqc)
            for u in range(chunk):
                p = c * chunk + u
                sl = slice(u * LANES, (u + 1) * LANES)
                acc = acc + jnp.maximum(ra[:, sl], 0.0) * wt_ref[2 * p:2 * p + 1, :]
                acc = acc + jnp.maximum(rb[:, sl], 0.0) * wt_ref[2 * p + 1:2 * p + 2, :]
        acc = jnp.where((j * tk + krow) <= qpos, acc, NEG_INF)
        key_ref[j] = _sortable_key(acc)
        return 0

    lax.fori_loop(0, nkv, score_tile, 0)
    thr = _kth_largest(key_ref, nkv, topk, tk)
    need = _topk_need(key_ref, nkv, thr, topk, tk)
    qs = qs_ref[...]
    slt = slt_ref[...]

    def attn_tile(j, carry, diagonal):
        eq_before = carry[3]
        r0 = pl.multiple_of(j * tk, tk)
        sel, eq_before = _topk_select(key_ref[j], thr, need, eq_before, slt)
        if diagonal:
            sel = sel & ((j * tk + krow) <= qpos)
        bias = jnp.where(sel, 0.0, NEG_INF)
        s = _dot_tb(k_ref[0, pl.ds(r0, tk), :], qs)
        return _softmax_tile_t(s, bias, v_ref[0, pl.ds(r0, tk), :], carry[:3], B_HEADS) + (eq_before,)

    init = _softmax_init_t(B_HEADS) + (jnp.zeros((1, LANES), F32),)
    carry = lax.fori_loop(0, nkv - 1, lambda j, c: attn_tile(j, c, False), init)
    _, l, acc, _ = attn_tile(nkv - 1, carry, True)
    out = acc / jnp.maximum(l, 1e-30)
    for h in range(B_HEADS):
        o_ref[0, :, h * HEAD_DIM:(h + 1) * HEAD_DIM] = (
            out[:, h * Q_BLOCK:(h + 1) * Q_BLOCK].T.astype(o_ref.dtype))


def dsa_attention(rq3, riq3, proj3, rp3, rpi3, projb3, slt, *, topk, tk=512):
    B, T, _ = proj3.shape
    tk = min(tk, T)
    assert tk >= topk and T % tk == 0
    return pl.pallas_call(
        functools.partial(_dsa_kernel, tk=tk, topk=topk),
        grid=(B, T // Q_BLOCK),
        in_specs=[pl.BlockSpec((1, Q_BLOCK, IDX_HEADS * IDX_DIM), lambda b, i: (b, i, 0)),
                  pl.BlockSpec((1, Q_BLOCK, LANES), lambda b, i: (b, i, PB["b_iw"])),
                  pl.BlockSpec((1, Q_BLOCK, B_W), lambda b, i: (b, i, 0)),
                  pl.BlockSpec((1, T, LANES), lambda b, i: (b, 0, 0)),
                  pl.BlockSpec((1, T, LANES), lambda b, i: (b, 0, 1)),
                  pl.BlockSpec((1, T, HEAD_DIM), lambda b, i: (b, 0, RB["b_k"])),
                  pl.BlockSpec((1, T, HEAD_DIM), lambda b, i: (b, 0, PB["b_v"])),
                  pl.BlockSpec((tk, tk), lambda b, i: (0, 0))],
        out_specs=pl.BlockSpec((1, Q_BLOCK, B_W), lambda b, i: (b, i, 0)),
        out_shape=jax.ShapeDtypeStruct((B, T, B_W), BF16),
        scratch_shapes=[pltpu.VMEM((IDX_HEADS // 2 * Q_BLOCK, LANES), BF16),
                        pltpu.VMEM((B_HEADS * Q_BLOCK, HEAD_DIM), BF16),
                        pltpu.VMEM((LANES, Q_BLOCK), F32),
                        pltpu.VMEM((T // tk, tk, LANES), I32)],
        compiler_params=_cparams(("parallel", "arbitrary")),
        name="dsa_attention",
    )(riq3, proj3, rq3, rpi3, rpi3, rp3, projb3, slt)


def _compress_kernel(x_ref, pe_ref, w_ref, o_ref, *, n_slc):
    def window_sum(r, lo, hi):
        acc = jnp.zeros((n_slc, HEAD_DIM), F32)
        for l in range(lo, hi):
            off = (CMP_STRIDE * r + l) % SLC_BLOCK
            rows = x_ref[0, pl.ds(off, n_slc, stride=SLC_BLOCK), :] + pe_ref[l:l + 1, :]
            acc = acc + _dot(rows.astype(BF16), w_ref[l])
        return acc

    for r in range(CMP_PER_SLC):
        inside = min(CMP_BLOCK, SLC_BLOCK - CMP_STRIDE * r)
        acc = window_sum(r, 0, inside)
        if inside < CMP_BLOCK:
            acc = acc + pltpu.roll(window_sum(r, inside, CMP_BLOCK), n_slc - 1, axis=0)
        o_ref[0, 0, r * n_slc:(r + 1) * n_slc, :] = acc.astype(o_ref.dtype)


def nsa_compress(proj3, col_block, pe, w3):
    B, T, _ = proj3.shape
    n_slc = T // SLC_BLOCK
    return pl.pallas_call(
        functools.partial(_compress_kernel, n_slc=n_slc),
        grid=(B, C_GROUPS),
        in_specs=[pl.BlockSpec((1, T, HEAD_DIM), lambda b, g: (b, 0, col_block + g)),
                  pl.BlockSpec((CMP_BLOCK, HEAD_DIM), lambda b, g: (0, 0)),
                  pl.BlockSpec((CMP_BLOCK, HEAD_DIM, HEAD_DIM), lambda b, g: (0, 0, 0))],
        out_specs=pl.BlockSpec((1, 1, CMP_PER_SLC * n_slc, HEAD_DIM), lambda b, g: (b, g, 0, 0)),
        out_shape=jax.ShapeDtypeStruct((B, C_GROUPS, CMP_PER_SLC * n_slc, HEAD_DIM), BF16),
        compiler_params=_cparams(("parallel", "parallel")),
        name="nsa_compress",
    )(proj3, pe, w3)


def _nsa_kernel(q_ref, kc_ref, vc_ref, ks_ref, vs_ref, kw_ref, vw_ref, g_ref, gb_ref,
                slt_ref, o_ref, qs_ref, gt_ref, key_ref, sb_ref, *, tk, n_slc, ksel):
    g = pl.program_id(1)
    i = pl.program_id(2)
    t0 = i * Q_BLOCK
    nh = C_GROUP_HEADS
    for h in range(nh):
        qs_ref[h * Q_BLOCK:(h + 1) * Q_BLOCK, :] = q_ref[0, :, h * HEAD_DIM:(h + 1) * HEAD_DIM]
    qs = qs_ref[...]
    gt_ref[...] = jax.nn.sigmoid(g_ref[0] + gb_ref[...]).T

    def lane_qpos(rows):
        return t0 + lax.broadcasted_iota(I32, (rows, Q_BLOCK), 1)

    nc = CMP_PER_SLC * n_slc
    crow = lax.broadcasted_iota(I32, (nc, Q_BLOCK), 0)
    cend = (crow % n_slc) * SLC_BLOCK + (crow // n_slc) * CMP_STRIDE + (CMP_BLOCK - 1)
    bias_c = jnp.where(cend <= lane_qpos(nc), 0.0, NEG_INF)
    has_cmp = lane_qpos(1) >= CMP_BLOCK - 1
    s_c = _dot_tb(kc_ref[0, 0], qs)
    ph = jnp.zeros((nc, Q_BLOCK), F32)
    p_heads = []
    for h in range(nh):
        sh = s_c[:, h * Q_BLOCK:(h + 1) * Q_BLOCK] + bias_c
        e = jnp.exp2(sh - jnp.max(sh, axis=0, keepdims=True))
        den = jnp.maximum(jnp.sum(e, axis=0, keepdims=True), 1e-30)
        pn = e * jnp.where(has_cmp, 1.0 / den, 0.0)
        ph = ph + pn
        p_heads.append(pn.astype(BF16))
    o_cmp = _dot_ta(vc_ref[0, 0], jnp.concatenate(p_heads, axis=1))

    imp = ph[0:n_slc, :]
    for r in range(1, CMP_PER_SLC):
        imp = imp + ph[r * n_slc:(r + 1) * n_slc, :]
    blk = lax.broadcasted_iota(I32, (n_slc, Q_BLOCK), 0)
    qp = lane_qpos(n_slc)
    cur = qp // SLC_BLOCK
    forced = (blk == 0) | (blk == cur) | (blk == cur - 1)
    imp = jnp.where(blk * SLC_BLOCK <= qp, imp + jnp.where(forced, FORCE_BONUS, 0.0), NEG_INF)
    key_ref[0] = _sortable_key(imp)
    thr = _kth_largest(key_ref, 1, ksel, n_slc)
    need = _topk_need(key_ref, 1, thr, ksel, n_slc)
    sel, _ = _topk_select(key_ref[0], thr, need, jnp.zeros((1, Q_BLOCK), F32), slt_ref[...])
    sb_ref[...] = jnp.where(sel, 0.0, NEG_INF)

    nkv = (t0 + Q_BLOCK + tk - 1) // tk
    blocks_per_tile = tk // SLC_BLOCK

    def sel_tile(j, carry, diagonal):
        r0 = pl.multiple_of(j * tk, tk)
        bias = jnp.concatenate(
            [jnp.broadcast_to(sb_ref[pl.ds(j * blocks_per_tile + u, 1), :], (SLC_BLOCK, Q_BLOCK))
             for u in range(blocks_per_tile)], axis=0)
        if diagonal:
            kpos = j * tk + lax.broadcasted_iota(I32, (tk, Q_BLOCK), 0)
            bias = jnp.where(kpos <= lane_qpos(tk), bias, NEG_INF)
        s = _dot_tb(ks_ref[0, pl.ds(r0, tk), :], qs)
        return _softmax_tile_t(s, bias, vs_ref[0, pl.ds(r0, tk), :], carry, nh)

    carry = lax.fori_loop(0, nkv - 1, lambda j, c: sel_tile(j, c, False), _softmax_init_t(nh))
    _, l_s, acc_s = sel_tile(nkv - 1, carry, True)
    o_sel = acc_s / jnp.maximum(l_s, 1e-30)

    span = WINDOW + Q_BLOCK
    start = pl.multiple_of(jnp.maximum(t0 - WINDOW, 0), Q_BLOCK)
    dist = lane_qpos(span) - (start + lax.broadcasted_iota(I32, (span, Q_BLOCK), 0))
    bias_w = jnp.where((dist >= 0) & (dist < WINDOW), 0.0, NEG_INF)
    s_w = _dot_tb(kw_ref[0, pl.ds(start, span), :], qs)
    _, l_w, acc_w = _softmax_tile_t(s_w, bias_w, vw_ref[0, pl.ds(start, span), :],
                                    _softmax_init_t(nh), nh)
    o_win = acc_w / jnp.maximum(l_w, 1e-30)

    for h in range(nh):
        sl = slice(h * Q_BLOCK, (h + 1) * Q_BLOCK)
        row = 3 * (g * nh + h)
        o = (gt_ref[pl.ds(row, 1), :] * o_cmp[:, sl]
             + gt_ref[pl.ds(row + 1, 1), :] * o_sel[:, sl]
             + gt_ref[pl.ds(row + 2, 1), :] * o_win[:, sl])
        o_ref[0, :, h * HEAD_DIM:(h + 1) * HEAD_DIM] = o.T.astype(o_ref.dtype)


def nsa_attention(rp3, projb3, proj3, kc, vc, gate_bias_row, slt, *, tk=512):
    B, T, _ = proj3.shape
    tk = min(tk, T)
    n_slc = T // SLC_BLOCK
    ksel = min(SLC_TOPK, n_slc)
    gw = C_GROUP_HEADS * HEAD_DIM
    assert T >= WINDOW + Q_BLOCK and T % tk == 0 and tk % SLC_BLOCK == 0

    def seq(col):
        return pl.BlockSpec((1, T, HEAD_DIM), lambda b, g, i, col=col: (b, 0, col + g))

    cmp_spec = pl.BlockSpec((1, 1, CMP_PER_SLC * n_slc, HEAD_DIM), lambda b, g, i: (b, g, 0, 0))
    return pl.pallas_call(
        functools.partial(_nsa_kernel, tk=tk, n_slc=n_slc, ksel=ksel),
        grid=(B, C_GROUPS, T // Q_BLOCK),
        in_specs=[pl.BlockSpec((1, Q_BLOCK, gw), lambda b, g, i: (b, i, g)),
                  cmp_spec, cmp_spec,
                  seq(RB["c_ks"]), seq(PB["c_vs"]), seq(RB["c_kw"]), seq(PB["c_vw"]),
                  pl.BlockSpec((1, Q_BLOCK, LANES), lambda b, g, i: (b, i, PB["c_g"])),
                  pl.BlockSpec((1, LANES), lambda b, g, i: (0, 0)),
                  pl.BlockSpec((n_slc, n_slc), lambda b, g, i: (0, 0))],
        out_specs=pl.BlockSpec((1, Q_BLOCK, gw), lambda b, g, i: (b, i, g)),
        out_shape=jax.ShapeDtypeStruct((B, T, C_W), BF16),
        scratch_shapes=[pltpu.VMEM((C_GROUP_HEADS * Q_BLOCK, HEAD_DIM), BF16),
                        pltpu.VMEM((LANES, Q_BLOCK), F32),
                        pltpu.VMEM((1, n_slc, Q_BLOCK), I32),
                        pltpu.VMEM((n_slc, Q_BLOCK), F32)],
        compiler_params=_cparams(("parallel", "parallel", "arbitrary")),
        name="nsa_attention",
    )(rp3, kc, vc, rp3, projb3, rp3, projb3, proj3, gate_bias_row, slt)


def _pad_cols(w, width):
    return jnp.pad(w, ((0, 0), (0, width - w.shape[1])))


def _permute_w_in(w_in):
    offs = np.concatenate([[0], np.cumsum(IN_WIDTHS)])
    part = {n: w_in[:, offs[k]:offs[k + 1]] for k, n in enumerate(IN_NAMES)}
    for name in ("a_q", "c_q"):
        part[name] = part[name] * QK_FOLD
    zeros64 = jnp.zeros((w_in.shape[0], IDX_DIM), w_in.dtype)
    order = sorted(PB, key=PB.get)
    cols = []
    for name, nxt in zip(order, order[1:] + [None]):
        end = (PB[nxt] if nxt else P_BLOCKS) * LANES
        if name == "ik_lo":
            blk = jnp.concatenate([part["b_ik"], zeros64], axis=1)
        elif name == "ik_hi":
            blk = jnp.concatenate([zeros64, part["b_ik"]], axis=1)
        else:
            blk = part[name]
        cols.append(_pad_cols(blk, end - PB[name] * LANES))
    return jnp.concatenate(cols, axis=1).astype(BF16)


def _lane_row(v):
    return _pad_cols(v.reshape(1, -1).astype(F32), LANES)


def kernel(x, positions, g_mix_pre, g_mix_post, g_ffn_pre, g_ffn_post, w_in, fox_forget_bias,
           dsa_q_norm, w_dsa_uq, w_dsa_iq, nsa_cmp_pos_k, nsa_cmp_pos_v, w_nsa_cmp_k,
           w_nsa_cmp_v, nsa_gate_bias, w_gate_up, b_gate, w_branch, w_out, w_ffn_gate,
           w_ffn_up, w_ffn_down):
    B, T, D = x.shape
    depth = w_in.shape[0]
    M = B * T
    topk_b = min(IDX_TOPK_MAX, T // 4)
    n_slc = T // SLC_BLOCK
    tk = min(512, T)

    tab128, half128 = rope_tables(positions, HEAD_DIM)
    tab64, half64 = rope_tables(positions, IDX_DIM)
    slt_tk = strict_lower(tk)
    slt_blk = strict_lower(n_slc)
    rope_src = ([PB["c_q"] + h for h in range(C_HEADS)] + [PB["b_k"]]
                + [PB["c_ks"], PB["c_ks"] + 1, PB["c_kw"], PB["c_kw"] + 1])
    d_ff = w_ffn_gate.shape[-1]
    ff_tn = next(t for t in (512, 256, 128) if d_ff % t == 0)
    ff_tk = d_ff // 2 if (d_ff // 2) % LANES == 0 else d_ff

    xf = x.reshape(M, D)
    h = rmsnorm(xf, g_mix_pre[0], width=D)
    for l in range(depth):
        proj, projb = matmul(h, _permute_w_in(w_in[l]), (F32, BF16))
        proj3 = proj.reshape(B, T, -1)
        projb3 = projb.reshape(B, T, -1)

        f, ft = fox_gate(proj3, _lane_row(fox_forget_bias[l]))
        o_a = fox_attention(projb3, f, ft)

        cqn = rmsnorm(proj, dsa_q_norm[l], width=B_Q_LORA, col_block=PB["b_cq"] * LANES // B_Q_LORA)
        w_q = jnp.concatenate([w_dsa_uq[l] * QK_FOLD, w_dsa_iq[l]], axis=1).astype(BF16)
        (qiq,) = matmul(cqn, w_q, (F32,))
        rq = rope(qiq, list(range(B_HEADS)), tab128, shift=half128)
        riq = rope(qiq, list(range(B_HEADS, B_HEADS + IDX_HEADS // 2)), tab64, shift=half64)
        rp3 = rope(proj, rope_src, tab128, shift=half128).reshape(B, T, -1)
        rpi3 = rope(proj, [PB["ik_lo"], PB["ik_hi"]], tab64, shift=half64).reshape(B, T, -1)
        o_b = dsa_attention(rq.reshape(B, T, -1), riq.reshape(B, T, -1), proj3, rp3, rpi3, projb3,
                            slt_tk, topk=topk_b, tk=tk)

        kc = nsa_compress(proj3, PB["c_kc"], nsa_cmp_pos_k[l],
                          w_nsa_cmp_k[l].reshape(CMP_BLOCK, HEAD_DIM, HEAD_DIM).astype(BF16))
        vc = nsa_compress(proj3, PB["c_vc"], nsa_cmp_pos_v[l],
                          w_nsa_cmp_v[l].reshape(CMP_BLOCK, HEAD_DIM, HEAD_DIM).astype(BF16))
        o_c = nsa_attention(rp3, projb3, proj3, kc, vc, _lane_row(nsa_gate_bias[l]),
                            slt_blk, tk=min(1024, T))

        wbr = w_branch[l].astype(BF16)
        y = merge(projb, o_a.reshape(M, -1), o_b.reshape(M, -1), o_c.reshape(M, -1),
                  w_gate_up[l].astype(BF16), b_gate[l].reshape(1, -1),
                  wbr[:A_W], wbr[A_W:A_W + B_W], wbr[A_W + B_W:])
        (mix,) = matmul(y, w_out[l].astype(BF16), (F32,))
        xf, h = resnorm(xf, mix, g_mix_post[l], g_ffn_pre[l])

        u = ffn_up(h, w_ffn_gate[l].astype(BF16), w_ffn_up[l].astype(BF16), tn=ff_tn)
        ffn = matmul_ksplit(u, w_ffn_down[l].astype(BF16), F32, tm=1024, tn=512, tk=ff_tk)
        if l + 1 < depth:
            xf, h = resnorm(xf, ffn, g_ffn_post[l], g_mix_pre[l + 1])
        else:
            xf = resnorm(xf, ffn, g_ffn_post[l])
    return xf.reshape(B, T, D)
```

```python
import functools

import jax
import jax.numpy as jnp
import numpy as np
from jax import lax
from jax.experimental import pallas as pl
from jax.experimental.pallas import tpu as pltpu

F32 = jnp.float32
BF16 = jnp.bfloat16
I32 = jnp.int32
I16 = jnp.int16

HEAD_DIM = 128
LANES = 128
ROPE_THETA = 500000.0
ROT_FRACTION = 4
NORM_EPS = 1e-6
NEG_INF = -1e30
INT_MIN = -(2 ** 31)

A_HEADS = 12
A_W = A_HEADS * HEAD_DIM
B_HEADS = 8
B_W = B_HEADS * HEAD_DIM
B_Q_LORA = 1024
IDX_HEADS = 32
IDX_DIM = 64
IDX_TOPK_MAX = 256
IDX_SCALE = (IDX_HEADS * IDX_DIM) ** -0.5
C_HEADS = 12
C_W = C_HEADS * HEAD_DIM
C_GROUPS = 2
C_GROUP_HEADS = C_HEADS // C_GROUPS
C_KV_W = C_GROUPS * HEAD_DIM
CMP_BLOCK = 32
CMP_STRIDE = 16
SLC_BLOCK = 64
SLC_TOPK = 16
CMP_PER_SLC = SLC_BLOCK // CMP_STRIDE
WINDOW = 512
FORCE_BONUS = 1e4
GATE_RANK = 512
Q_BLOCK = 128
LOG2E = 1.4426950408889634
QK_FOLD = HEAD_DIM ** -0.5 * LOG2E

IN_WIDTHS = (A_W, A_W, A_W, A_HEADS,
             B_Q_LORA, HEAD_DIM, HEAD_DIM, IDX_DIM, IDX_HEADS,
             C_W, C_KV_W, C_KV_W, C_KV_W, C_KV_W, C_KV_W, C_KV_W, 3 * C_HEADS,
             GATE_RANK)
IN_NAMES = ("a_q", "a_k", "a_v", "a_f", "b_cq", "b_k", "b_v", "b_ik", "b_iw",
            "c_q", "c_kc", "c_vc", "c_ks", "c_vs", "c_kw", "c_vw", "c_g", "z")

PB = dict(c_q=0, b_k=12, c_ks=13, c_kw=15, ik_lo=17, ik_hi=18, b_v=19, z=20, b_cq=24,
          a_q=32, a_k=44, a_v=56, c_kc=68, c_vc=70, c_vs=72, c_vw=74,
          a_f=76, b_iw=77, c_g=78)
P_BLOCKS = 80
ROPE_BLOCKS = 19
QB = dict(iq=0, q=IDX_HEADS * IDX_DIM // LANES)

VMEM_LIMIT = 56 * 1024 * 1024


def _cparams(sem, vmem=VMEM_LIMIT):
    return pltpu.CompilerParams(dimension_semantics=sem, vmem_limit_bytes=vmem)


def _dot(a, b):
    return jnp.dot(a, b, preferred_element_type=F32)


def _dot_tb(a, b):
    return lax.dot_general(a, b, (((1,), (1,)), ((), ())), preferred_element_type=F32)


def _dot_ta(a, b):
    return lax.dot_general(a, b, (((0,), (0,)), ((), ())), preferred_element_type=F32)


def _rmsnorm_kernel(x_ref, g_ref, o_ref):
    x = x_ref[...].astype(F32)
    y = x * lax.rsqrt(jnp.mean(x * x, axis=-1, keepdims=True) + NORM_EPS)
    o_ref[...] = (y * g_ref[...]).astype(o_ref.dtype)


def rmsnorm(x, g, *, width, col_block=0, tm=512):
    M = x.shape[0]
    tm = min(tm, M)
    return pl.pallas_call(
        _rmsnorm_kernel,
        grid=(M // tm,),
        in_specs=[pl.BlockSpec((tm, width), lambda i: (i, col_block)),
                  pl.BlockSpec((1, width), lambda i: (0, 0))],
        out_specs=pl.BlockSpec((tm, width), lambda i: (i, 0)),
        out_shape=jax.ShapeDtypeStruct((M, width), BF16),
        compiler_params=_cparams(("parallel",)),
        name="rmsnorm",
    )(x, g.reshape(1, width).astype(F32))


def _resnorm_kernel(x_ref, m_ref, g1_ref, g2_ref, xo_ref, ho_ref):
    m = m_ref[...]
    y = m * lax.rsqrt(jnp.mean(m * m, axis=-1, keepdims=True) + NORM_EPS) * g1_ref[...]
    xn = x_ref[...] + y
    xo_ref[...] = xn
    h = xn * lax.rsqrt(jnp.mean(xn * xn, axis=-1, keepdims=True) + NORM_EPS) * g2_ref[...]
    ho_ref[...] = h.astype(ho_ref.dtype)


def _resnorm_last_kernel(x_ref, m_ref, g1_ref, xo_ref):
    m = m_ref[...]
    y = m * lax.rsqrt(jnp.mean(m * m, axis=-1, keepdims=True) + NORM_EPS) * g1_ref[...]
    xo_ref[...] = x_ref[...] + y


def resnorm(x, m, g1, g2=None, *, tm=256):
    M, D = x.shape
    tm = min(tm, M)
    row = pl.BlockSpec((tm, D), lambda i: (i, 0))
    gain = pl.BlockSpec((1, D), lambda i: (0, 0))
    if g2 is None:
        return pl.pallas_call(
            _resnorm_last_kernel, grid=(M // tm,),
            in_specs=[row, row, gain], out_specs=row,
            out_shape=jax.ShapeDtypeStruct((M, D), F32),
            compiler_params=_cparams(("parallel",)), name="resnorm_last",
        )(x, m, g1.reshape(1, D))
    return pl.pallas_call(
        _resnorm_kernel, grid=(M // tm,),
        in_specs=[row, row, gain, gain], out_specs=[row, row],
        out_shape=[jax.ShapeDtypeStruct((M, D), F32), jax.ShapeDtypeStruct((M, D), BF16)],
        compiler_params=_cparams(("parallel",)), name="resnorm",
    )(x, m, g1.reshape(1, D), g2.reshape(1, D))


def _mm_kernel(x_ref, w_ref, *o_refs):
    acc = _dot(x_ref[...], w_ref[...])
    for o_ref in o_refs:
        o_ref[...] = acc.astype(o_ref.dtype)


def _layer_cols(layer, K, tn):
    return pl.BlockSpec((None, K, tn), lambda i, j: (layer, 0, j))


def matmul(x, w, layer, out_dtypes, *, tm=1024, tn=512):
    M, K = x.shape
    N = w.shape[2]
    tm, tn = min(tm, M), min(tn, N)
    outs = pl.pallas_call(
        _mm_kernel,
        grid=(M // tm, N // tn),
        in_specs=[pl.BlockSpec((tm, K), lambda i, j: (i, 0)), _layer_cols(layer, K, tn)],
        out_specs=[pl.BlockSpec((tm, tn), lambda i, j: (i, j)) for _ in out_dtypes],
        out_shape=[jax.ShapeDtypeStruct((M, N), dt) for dt in out_dtypes],
        compiler_params=_cparams(("parallel", "arbitrary")),
        name="matmul",
    )(x, w)
    return outs


def _mm_ksplit_kernel(x_ref, w_ref, o_ref, acc_ref):
    k = pl.program_id(2)

    @pl.when(k == 0)
    def _():
        acc_ref[...] = jnp.zeros_like(acc_ref)

    acc_ref[...] += _dot(x_ref[...], w_ref[...])

    @pl.when(k == pl.num_programs(2) - 1)
    def _():
        o_ref[...] = acc_ref[...].astype(o_ref.dtype)


def matmul_ksplit(x, w, layer, out_dtype, *, tm, tn, tk):
    M, K = x.shape
    N = w.shape[2]
    tm, tn, tk = min(tm, M), min(tn, N), min(tk, K)
    return pl.pallas_call(
        _mm_ksplit_kernel,
        grid=(M // tm, N // tn, K // tk),
        in_specs=[pl.BlockSpec((tm, tk), lambda i, j, k: (i, k)),
                  pl.BlockSpec((None, tk, tn), lambda i, j, k: (layer, k, j))],
        out_specs=pl.BlockSpec((tm, tn), lambda i, j, k: (i, j)),
        out_shape=jax.ShapeDtypeStruct((M, N), out_dtype),
        scratch_shapes=[pltpu.VMEM((tm, tn), F32)],
        compiler_params=_cparams(("parallel", "arbitrary", "arbitrary")),
        name="matmul_ksplit",
    )(x, w)


def _ffn_up_kernel(h_ref, wg_ref, wu_ref, o_ref):
    h = h_ref[...]
    g = _dot(h, wg_ref[...])
    u = _dot(h, wu_ref[...])
    o_ref[...] = (g * jax.nn.sigmoid(g) * u).astype(o_ref.dtype)


def ffn_up(h, wg, wu, layer, *, tm=1024, tn=256):
    M, K = h.shape
    N = wg.shape[2]
    tm, tn = min(tm, M), min(tn, N)
    return pl.pallas_call(
        _ffn_up_kernel,
        grid=(M // tm, N // tn),
        in_specs=[pl.BlockSpec((tm, K), lambda i, j: (i, 0)),
                  _layer_cols(layer, K, tn), _layer_cols(layer, K, tn)],
        out_specs=pl.BlockSpec((tm, tn), lambda i, j: (i, j)),
        out_shape=jax.ShapeDtypeStruct((M, N), BF16),
        compiler_params=_cparams(("parallel", "arbitrary")),
        name="ffn_up",
    )(h, wg, wu)


def _merge_kernel(z_ref, oa_ref, ob_ref, oc_ref, wga_ref, wgb_ref, wgc_ref,
                  bga_ref, bgb_ref, bgc_ref, wbr_ref, y_ref):
    z = z_ref[...]

    def branch(o_ref, row0, wg_ref, bg_ref):
        gate = jax.nn.sigmoid(_dot(z, wg_ref[...]) + bg_ref[...])
        rows = o_ref.shape[1]
        return gate * _dot(o_ref[...], wbr_ref[row0:row0 + rows, :])

    y = (branch(oa_ref, 0, wga_ref, bga_ref)
         + branch(ob_ref, A_W, wgb_ref, bgb_ref)
         + branch(oc_ref, A_W + B_W, wgc_ref, bgc_ref))
    y_ref[...] = y.astype(y_ref.dtype)


def merge(projb, o_a, o_b, o_c, w_gate_up, b_gate, w_branch, layer, *, tm=1024, tn=512):
    M = projb.shape[0]
    KB, D = w_branch.shape[1:]
    tm, tn = min(tm, M), min(tn, D)
    nj = D // tn
    z_blk = PB["z"] * LANES // GATE_RANK

    def full_k(o):
        return pl.BlockSpec((tm, o.shape[1]), lambda i, j: (i, 0))

    def gate_w(b):
        return pl.BlockSpec((None, GATE_RANK, tn), lambda i, j, b=b: (layer, 0, b * nj + j))

    def gate_b(b):
        return pl.BlockSpec((None, 1, tn), lambda i, j, b=b: (layer, 0, b * nj + j))

    return pl.pallas_call(
        _merge_kernel,
        grid=(M // tm, nj),
        in_specs=[pl.BlockSpec((tm, GATE_RANK), lambda i, j: (i, z_blk)),
                  full_k(o_a), full_k(o_b), full_k(o_c),
                  gate_w(0), gate_w(1), gate_w(2), gate_b(0), gate_b(1), gate_b(2),
                  _layer_cols(layer, KB, tn)],
        out_specs=pl.BlockSpec((tm, tn), lambda i, j: (i, j)),
        out_shape=jax.ShapeDtypeStruct((M, D), BF16),
        compiler_params=_cparams(("parallel", "arbitrary")),
        name="merge",
    )(projb, o_a, o_b, o_c, w_gate_up, w_gate_up, w_gate_up, b_gate, b_gate, b_gate, w_branch)


def _rope_kernel(x_ref, *refs, plan):
    o_ref = refs[-1]
    tables = refs[:-1]
    for c, (ts, half) in enumerate(plan):
        sl = slice(c * LANES, (c + 1) * LANES)
        x = x_ref[:, sl]
        cos, sin_up, sin_dn = (tables[3 * ts + t][...] for t in range(3))
        up = pltpu.roll(x, LANES - half, axis=1)
        dn = pltpu.roll(x, half, axis=1)
        o_ref[:, sl] = (x * cos + up * sin_up + dn * sin_dn).astype(o_ref.dtype)


def rope(x, plan, table_sets, *, tm=512):
    M = x.shape[0]
    tm = min(tm, M)
    width = len(plan) * LANES
    tab = pl.BlockSpec((tm, LANES), lambda i: (i, 0))
    flat = [t for ts in table_sets for t in ts]
    return pl.pallas_call(
        functools.partial(_rope_kernel, plan=tuple(plan)),
        grid=(M // tm,),
        in_specs=[pl.BlockSpec((tm, width), lambda i: (i, 0))] + [tab] * len(flat),
        out_specs=pl.BlockSpec((tm, width), lambda i: (i, 0)),
        out_shape=jax.ShapeDtypeStruct((M, width), BF16),
        compiler_params=_cparams(("parallel",)),
        name="rope",
    )(x, *flat)


def rope_tables(positions, head_dim):
    B, T = positions.shape
    rot = head_dim // ROT_FRACTION
    half = rot // 2
    inv_freq = ROPE_THETA ** (-jnp.arange(half, dtype=F32) / half)
    ang = positions.astype(F32)[:, :, None] * inv_freq
    cos, sin = jnp.cos(ang), jnp.sin(ang)
    rest = head_dim - rot
    zh = jnp.zeros((B, T, half), F32)
    c = jnp.concatenate([cos, cos, jnp.ones((B, T, rest), F32)], axis=-1)
    s1 = jnp.concatenate([-sin, zh, jnp.zeros((B, T, rest), F32)], axis=-1)
    s2 = jnp.concatenate([zh, sin, jnp.zeros((B, T, rest), F32)], axis=-1)
    reps = LANES // head_dim
    return tuple(jnp.tile(t, (1, 1, reps)).reshape(B * T, LANES) for t in (c, s1, s2)), half


def _fox_gate_kernel(a_ref, b_ref, f_ref, ft_ref, carry_ref, *, tb):
    @pl.when(pl.program_id(1) == 0)
    def _():
        carry_ref[...] = jnp.zeros_like(carry_ref)

    x = a_ref[0] + b_ref[...]
    log_f = jnp.minimum(x, 0.0) - jnp.log1p(jnp.exp(-jnp.abs(x)))
    row = lax.broadcasted_iota(I32, (tb, tb), 0)
    col = lax.broadcasted_iota(I32, (tb, tb), 1)
    tri = (col <= row).astype(F32)
    cum = jnp.dot(tri, log_f, preferred_element_type=F32,
                  precision=lax.Precision.HIGHEST) + carry_ref[...]
    cum2 = cum * LOG2E
    f_ref[0] = cum2
    ft_ref[0] = cum2.T[:16, :]
    carry_ref[...] = cum[tb - 1:tb, :]


def fox_gate(proj3, bias_row, *, tb=256):
    B, T, _ = proj3.shape
    tb = min(tb, T)
    return pl.pallas_call(
        functools.partial(_fox_gate_kernel, tb=tb),
        grid=(B, T // tb),
        in_specs=[pl.BlockSpec((1, tb, LANES), lambda b, j: (b, j, PB["a_f"])),
                  pl.BlockSpec((1, LANES), lambda b, j: (0, 0))],
        out_specs=[pl.BlockSpec((1, tb, LANES), lambda b, j: (b, j, 0)),
                   pl.BlockSpec((1, 16, tb), lambda b, j: (b, 0, j))],
        out_shape=[jax.ShapeDtypeStruct((B, T, LANES), F32),
                   jax.ShapeDtypeStruct((B, 16, T), F32)],
        scratch_shapes=[pltpu.VMEM((1, LANES), F32)],
        compiler_params=_cparams(("parallel", "arbitrary")),
        name="fox_gate",
    )(proj3, bias_row)


FOX_HEADS_PER_STEP = 1


def _fox_kernel(q_ref, k_ref, v_ref, f_ref, ft_ref, o_ref, *, tq, tk):
    nh = FOX_HEADS_PER_STEP
    hg = pl.program_id(1)
    i = pl.program_id(2)
    lane = lax.broadcasted_iota(I32, (tq, LANES), 1)
    f_rows = f_ref[0]
    heads = [slice(g * HEAD_DIM, (g + 1) * HEAD_DIM) for g in range(nh)]
    q = [q_ref[0, :, sl] for sl in heads]
    fq = [jnp.sum(jnp.where(lane == hg * nh + g, f_rows, 0.0), axis=1, keepdims=True)
          for g in range(nh)]
    nkv = (i * tq + tq + tk - 1) // tk

    def update(j, carry, diagonal):
        r0 = pl.multiple_of(j * tk, tk)
        out = []
        for g in range(nh):
            m, l, acc = carry[g]
            fk = ft_ref[0, g, pl.ds(j, 1), :]
            t = _dot_tb(q[g], k_ref[0, pl.ds(r0, tk), heads[g]]) - fk
            if diagonal:
                qpos = i * tq + lax.broadcasted_iota(I32, (tq, tk), 0)
                kpos = j * tk + lax.broadcasted_iota(I32, (tq, tk), 1)
                t = jnp.where(kpos <= qpos, t, NEG_INF)
            m_new = jnp.maximum(m, jnp.max(t, axis=1, keepdims=True) + fq[g])
            p = jnp.exp2(t + (fq[g] - m_new))
            alpha = jnp.exp2(m - m_new)
            l = alpha * l + jnp.sum(p, axis=1, keepdims=True)
            acc = alpha * acc + _dot(p.astype(BF16), v_ref[0, pl.ds(r0, tk), heads[g]])
            out.append((m_new, l, acc))
        return tuple(out)

    init = tuple((jnp.full((tq, 1), NEG_INF, F32), jnp.zeros((tq, 1), F32),
                  jnp.zeros((tq, HEAD_DIM), F32)) for _ in range(nh))
    carry = lax.fori_loop(0, nkv - 1, lambda j, c: update(j, c, False), init)
    carry = update(nkv - 1, carry, True)
    for g in range(nh):
        _, l, acc = carry[g]
        o_ref[0, :, heads[g]] = (acc / jnp.maximum(l, 1e-30)).astype(o_ref.dtype)


def fox_attention(projb3, f, ft, *, tq=1024, tk=1024):
    B, T, _ = projb3.shape
    tq, tk = min(tq, T), min(tk, T)
    nh = FOX_HEADS_PER_STEP
    gw = nh * HEAD_DIM
    assert tk % tq == 0 and A_HEADS % nh == 0
    assert PB["a_q"] % nh == 0 and PB["a_k"] % nh == 0 and PB["a_v"] % nh == 0
    ft4 = ft.reshape(B, 16, T // tk, tk)
    return pl.pallas_call(
        functools.partial(_fox_kernel, tq=tq, tk=tk),
        grid=(B, A_HEADS // nh, T // tq),
        in_specs=[pl.BlockSpec((1, tq, gw), lambda b, h, i: (b, i, PB["a_q"] // nh + h)),
                  pl.BlockSpec((1, T, gw), lambda b, h, i: (b, 0, PB["a_k"] // nh + h)),
                  pl.BlockSpec((1, T, gw), lambda b, h, i: (b, 0, PB["a_v"] // nh + h)),
                  pl.BlockSpec((1, tq, LANES), lambda b, h, i: (b, i, 0)),
                  pl.BlockSpec((1, nh, T // tk, tk), lambda b, h, i: (b, h, 0, 0))],
        out_specs=pl.BlockSpec((1, tq, gw), lambda b, h, i: (b, i, h)),
        out_shape=jax.ShapeDtypeStruct((B, T, A_W), BF16),
        compiler_params=_cparams(("parallel", "parallel", "arbitrary")),
        name="fox_attention",
    )(projb3, projb3, projb3, f, ft4)


def _sortable_key(x):
    b = pltpu.bitcast(x, I32)
    return jnp.where(b >= 0, b, b ^ jnp.int32(0x7FFFFFFF))


def _count_ge(key_ref, ntiles, cand, rows):
    def tile(j, cnt):
        ge = jnp.where(key_ref[j] >= cand, 1, 0).astype(I32)
        return cnt + jnp.sum(ge.reshape(rows // 8, 8, LANES), axis=0)

    cnt = lax.fori_loop(0, ntiles, tile, jnp.zeros((8, LANES), I32))
    return jnp.sum(cnt, axis=0, keepdims=True)


def _kth_largest(key_ref, ntiles, k, rows):
    nonneg = _count_ge(key_ref, ntiles, jnp.zeros((1, LANES), I32), rows)
    base0 = jnp.where(nonneg >= k, 0, INT_MIN).astype(I32)

    def step(s, base):
        cand = base | lax.shift_left(jnp.int32(1), 30 - s)
        cnt = _count_ge(key_ref, ntiles, cand, rows)
        return jnp.where(cnt >= k, cand, base)

    return lax.fori_loop(0, 31, step, base0)


def _count_ge16(half_ref, ntiles, cand, rows):
    cand16 = cand.astype(I16)

    def tile(j, cnt):
        ge = jnp.where(half_ref[j] >= cand16, jnp.int16(1), jnp.int16(0))
        parts = [ge[r:r + 16, :] for r in range(0, rows, 16)]
        while len(parts) > 1:
            parts = [a + b for a, b in zip(parts[::2], parts[1::2])]
        return cnt + parts[0]

    cnt = lax.fori_loop(0, ntiles, tile, jnp.zeros((16, LANES), I16))
    return jnp.sum(cnt.astype(I32), axis=0, keepdims=True)


def _kth_largest16(half_ref, ntiles, k, rows):
    nonneg = _count_ge16(half_ref, ntiles, jnp.zeros((1, LANES), I32), rows)
    base0 = jnp.where(nonneg >= k, 0, -(2 ** 15)).astype(I32)

    def step(s, base):
        cand = base | lax.shift_left(jnp.int32(1), 14 - s)
        cnt = _count_ge16(half_ref, ntiles, cand, rows)
        return jnp.where(cnt >= k, cand, base)

    return lax.fori_loop(0, 15, step, base0)


def _kth_largest_split(key_ref, half_ref, ntiles, k, rows):
    t_hi = _kth_largest16(half_ref, ntiles, k, rows)
    above = _count_ge16(half_ref, ntiles, jnp.minimum(t_hi + 1, 2 ** 15 - 1), rows)
    above = jnp.where(t_hi == 2 ** 15 - 1, 0, above)

    def low_tile(j, _):
        key = key_ref[j]
        low = (key & 0xFFFF) - 2 ** 15
        half_ref[j] = jnp.where((key >> 16) == t_hi, low, -(2 ** 15)).astype(I16)
        return 0

    lax.fori_loop(0, ntiles, low_tile, 0)
    t_lo = _kth_largest16(half_ref, ntiles, k - above, rows)
    return lax.shift_left(t_hi, 16) | (t_lo + 2 ** 15)


def _topk_need(key_ref, ntiles, thr, k, rows):
    gt = _count_ge(key_ref, ntiles, thr + 1, rows)
    gt = jnp.where(thr == jnp.int32(2 ** 31 - 1), 0, gt)
    return (k - gt).astype(F32)


def _topk_select(key, thr, need, eq_before, slt):
    eq = key == thr
    eq_f = jnp.where(eq, 1.0, 0.0)
    rank = _dot(slt, eq_f.astype(BF16)) + eq_before
    sel = (key > thr) | (eq & (rank < need))
    return sel, eq_before + jnp.sum(eq_f, axis=0, keepdims=True)


def strict_lower(n):
    r = jnp.arange(n)
    return (r[None, :] < r[:, None]).astype(BF16)


def _softmax_tile_t(s, bias, v, carry, nheads):
    m, l, acc = carry
    ms, ls, ps = [], [], []
    for h in range(nheads):
        sl = slice(h * LANES, (h + 1) * LANES)
        sh = s[:, sl] + bias
        mh = jnp.maximum(m[:, sl], jnp.max(sh, axis=0, keepdims=True))
        ph = jnp.exp2(sh - mh)
        ms.append(mh)
        ls.append(jnp.sum(ph, axis=0, keepdims=True))
        ps.append(ph.astype(BF16))
    m_new = jnp.concatenate(ms, axis=1)
    alpha = jnp.exp2(m - m_new)
    l = alpha * l + jnp.concatenate(ls, axis=1)
    acc = alpha * acc + _dot_ta(v, jnp.concatenate(ps, axis=1))
    return m_new, l, acc


def _softmax_init_t(nheads):
    width = nheads * LANES
    return (jnp.full((1, width), NEG_INF, F32), jnp.zeros((1, width), F32),
            jnp.zeros((HEAD_DIM, width), F32))


def _dsa_kernel(iq_ref, iw_ref, q_ref, ika_ref, ikb_ref, k_ref, v_ref, slt_ref, o_ref,
                iqs_ref, qs_ref, wt_ref, key_ref, half_ref, *, tk, topk):
    i = pl.program_id(1)
    t0 = i * Q_BLOCK
    nkv = (t0 + Q_BLOCK + tk - 1) // tk
    npair = IDX_HEADS // 2
    for p in range(npair):
        iqs_ref[p * Q_BLOCK:(p + 1) * Q_BLOCK, :] = iq_ref[0, :, p * LANES:(p + 1) * LANES]
    for h in range(B_HEADS):
        qs_ref[h * Q_BLOCK:(h + 1) * Q_BLOCK, :] = q_ref[0, :, h * HEAD_DIM:(h + 1) * HEAD_DIM]
    wt_ref[...] = (iw_ref[0] * IDX_SCALE).T
    qpos = t0 + lax.broadcasted_iota(I32, (tk, LANES), 1)
    krow = lax.broadcasted_iota(I32, (tk, LANES), 0)
    chunk = 4

    def score_tile(j, _):
        r0 = pl.multiple_of(j * tk, tk)
        ika = ika_ref[0, pl.ds(r0, tk), :]
        ikb = ikb_ref[0, pl.ds(r0, tk), :]
        acc = jnp.zeros((tk, LANES), F32)
        for c in range(npair // chunk):
            iqc = iqs_ref[c * chunk * Q_BLOCK:(c + 1) * chunk * Q_BLOCK, :]
            ra = _dot_tb(ika, iqc)
            rb = _dot_tb(ikb, iqc)
            for u in range(chunk):
                p = c * chunk + u
                sl = slice(u * LANES, (u + 1) * LANES)
                acc = acc + jnp.maximum(ra[:, sl], 0.0) * wt_ref[2 * p:2 * p + 1, :]
                acc = acc + jnp.maximum(rb[:, sl], 0.0) * wt_ref[2 * p + 1:2 * p + 2, :]
        acc = jnp.where((j * tk + krow) <= qpos, acc, NEG_INF)
        key = _sortable_key(acc)
        key_ref[j] = key
        half_ref[j] = (key >> 16).astype(I16)
        return 0

    lax.fori_loop(0, nkv, score_tile, 0)
    thr = _kth_largest_split(key_ref, half_ref, nkv, topk, tk)
    need = _topk_need(key_ref, nkv, thr, topk, tk)
    qs = qs_ref[...]
    slt = slt_ref[...]

    def attn_tile(j, carry, diagonal):
        eq_before = carry[3]
        r0 = pl.multiple_of(j * tk, tk)
        sel, eq_before = _topk_select(key_ref[j], thr, need, eq_before, slt)
        if diagonal:
            sel = sel & ((j * tk + krow) <= qpos)
        bias = jnp.where(sel, 0.0, NEG_INF)
        s = _dot_tb(k_ref[0, pl.ds(r0, tk), :], qs)
        return _softmax_tile_t(s, bias, v_ref[0, pl.ds(r0, tk), :], carry[:3], B_HEADS) + (eq_before,)

    init = _softmax_init_t(B_HEADS) + (jnp.zeros((1, LANES), F32),)
    carry = lax.fori_loop(0, nkv - 1, lambda j, c: attn_tile(j, c, False), init)
    _, l, acc, _ = attn_tile(nkv - 1, carry, True)
    out = acc / jnp.maximum(l, 1e-30)
    for h in range(B_HEADS):
        o_ref[0, :, h * HEAD_DIM:(h + 1) * HEAD_DIM] = (
            out[:, h * Q_BLOCK:(h + 1) * Q_BLOCK].T.astype(o_ref.dtype))


def dsa_attention(rqiq3, proj3, rp3, projb3, slt, *, topk, tk=512):
    B, T, _ = proj3.shape
    tk = min(tk, T)
    assert tk >= topk and T % tk == 0
    return pl.pallas_call(
        functools.partial(_dsa_kernel, tk=tk, topk=topk),
        grid=(B, T // Q_BLOCK),
        in_specs=[pl.BlockSpec((1, Q_BLOCK, IDX_HEADS * IDX_DIM), lambda b, i: (b, i, QB["iq"])),
                  pl.BlockSpec((1, Q_BLOCK, LANES), lambda b, i: (b, i, PB["b_iw"])),
                  pl.BlockSpec((1, Q_BLOCK, B_W), lambda b, i: (b, i, QB["q"] * LANES // B_W)),
                  pl.BlockSpec((1, T, LANES), lambda b, i: (b, 0, PB["ik_lo"])),
                  pl.BlockSpec((1, T, LANES), lambda b, i: (b, 0, PB["ik_hi"])),
                  pl.BlockSpec((1, T, HEAD_DIM), lambda b, i: (b, 0, PB["b_k"])),
                  pl.BlockSpec((1, T, HEAD_DIM), lambda b, i: (b, 0, PB["b_v"])),
                  pl.BlockSpec((tk, tk), lambda b, i: (0, 0))],
        out_specs=pl.BlockSpec((1, Q_BLOCK, B_W), lambda b, i: (b, i, 0)),
        out_shape=jax.ShapeDtypeStruct((B, T, B_W), BF16),
        scratch_shapes=[pltpu.VMEM((IDX_HEADS // 2 * Q_BLOCK, LANES), BF16),
                        pltpu.VMEM((B_HEADS * Q_BLOCK, HEAD_DIM), BF16),
                        pltpu.VMEM((LANES, Q_BLOCK), F32),
                        pltpu.VMEM((T // tk, tk, LANES), I32),
                        pltpu.VMEM((T // tk, tk, LANES), I16)],
        compiler_params=_cparams(("parallel", "arbitrary")),
        name="dsa_attention",
    )(rqiq3, proj3, rqiq3, rp3, rp3, rp3, projb3, slt)


def _compress_kernel(x_ref, pe_ref, w_ref, o_ref, *, n_slc):
    def window_sum(r, lo, hi):
        acc = jnp.zeros((n_slc, HEAD_DIM), F32)
        for l in range(lo, hi):
            off = (CMP_STRIDE * r + l) % SLC_BLOCK
            rows = x_ref[0, pl.ds(off, n_slc, stride=SLC_BLOCK), :] + pe_ref[l:l + 1, :]
            acc = acc + _dot(rows.astype(BF16), w_ref[l])
        return acc

    for r in range(CMP_PER_SLC):
        inside = min(CMP_BLOCK, SLC_BLOCK - CMP_STRIDE * r)
        acc = window_sum(r, 0, inside)
        if inside < CMP_BLOCK:
            acc = acc + pltpu.roll(window_sum(r, inside, CMP_BLOCK), n_slc - 1, axis=0)
        o_ref[0, 0, r * n_slc:(r + 1) * n_slc, :] = acc.astype(o_ref.dtype)


def nsa_compress(proj3, col_block, pe, w4, layer):
    B, T, _ = proj3.shape
    n_slc = T // SLC_BLOCK
    return pl.pallas_call(
        functools.partial(_compress_kernel, n_slc=n_slc),
        grid=(B, C_GROUPS),
        in_specs=[pl.BlockSpec((1, T, HEAD_DIM), lambda b, g: (b, 0, col_block + g)),
                  pl.BlockSpec((None, CMP_BLOCK, HEAD_DIM), lambda b, g: (layer, 0, 0)),
                  pl.BlockSpec((None, CMP_BLOCK, HEAD_DIM, HEAD_DIM), lambda b, g: (layer, 0, 0, 0))],
        out_specs=pl.BlockSpec((1, 1, CMP_PER_SLC * n_slc, HEAD_DIM), lambda b, g: (b, g, 0, 0)),
        out_shape=jax.ShapeDtypeStruct((B, C_GROUPS, CMP_PER_SLC * n_slc, HEAD_DIM), BF16),
        compiler_params=_cparams(("parallel", "parallel")),
        name="nsa_compress",
    )(proj3, pe, w4)


def _nsa_kernel(q_ref, kc_ref, vc_ref, ks_ref, vs_ref, kw_ref, vw_ref, g_ref, gb_ref,
                slt_ref, o_ref, qs_ref, gt_ref, key_ref, sb_ref, *, tk, n_slc, ksel):
    g = pl.program_id(1)
    i = pl.program_id(2)
    t0 = i * Q_BLOCK
    nh = C_GROUP_HEADS
    for h in range(nh):
        qs_ref[h * Q_BLOCK:(h + 1) * Q_BLOCK, :] = q_ref[0, :, h * HEAD_DIM:(h + 1) * HEAD_DIM]
    qs = qs_ref[...]
    gt_ref[...] = jax.nn.sigmoid(g_ref[0] + gb_ref[...]).T

    def lane_qpos(rows):
        return t0 + lax.broadcasted_iota(I32, (rows, Q_BLOCK), 1)

    nc = CMP_PER_SLC * n_slc
    crow = lax.broadcasted_iota(I32, (nc, Q_BLOCK), 0)
    cend = (crow % n_slc) * SLC_BLOCK + (crow // n_slc) * CMP_STRIDE + (CMP_BLOCK - 1)
    bias_c = jnp.where(cend <= lane_qpos(nc), 0.0, NEG_INF)
    has_cmp = lane_qpos(1) >= CMP_BLOCK - 1
    s_c = _dot_tb(kc_ref[0, 0], qs)
    ph = jnp.zeros((nc, Q_BLOCK), F32)
    p_heads = []
    for h in range(nh):
        sh = s_c[:, h * Q_BLOCK:(h + 1) * Q_BLOCK] + bias_c
        e = jnp.exp2(sh - jnp.max(sh, axis=0, keepdims=True))
        den = jnp.maximum(jnp.sum(e, axis=0, keepdims=True), 1e-30)
        pn = e * jnp.where(has_cmp, 1.0 / den, 0.0)
        ph = ph + pn
        p_heads.append(pn.astype(BF16))
    o_cmp = _dot_ta(vc_ref[0, 0], jnp.concatenate(p_heads, axis=1))

    imp = ph[0:n_slc, :]
    for r in range(1, CMP_PER_SLC):
        imp = imp + ph[r * n_slc:(r + 1) * n_slc, :]
    blk = lax.broadcasted_iota(I32, (n_slc, Q_BLOCK), 0)
    qp = lane_qpos(n_slc)
    cur = qp // SLC_BLOCK
    forced = (blk == 0) | (blk == cur) | (blk == cur - 1)
    imp = jnp.where(blk * SLC_BLOCK <= qp, imp + jnp.where(forced, FORCE_BONUS, 0.0), NEG_INF)
    key_ref[0] = _sortable_key(imp)
    thr = _kth_largest(key_ref, 1, ksel, n_slc)
    need = _topk_need(key_ref, 1, thr, ksel, n_slc)
    sel, _ = _topk_select(key_ref[0], thr, need, jnp.zeros((1, Q_BLOCK), F32), slt_ref[...])
    sb_ref[...] = jnp.where(sel, 0.0, NEG_INF)

    nkv = (t0 + Q_BLOCK + tk - 1) // tk
    blocks_per_tile = tk // SLC_BLOCK

    def sel_tile(j, carry, diagonal):
        r0 = pl.multiple_of(j * tk, tk)
        bias = jnp.concatenate(
            [jnp.broadcast_to(sb_ref[pl.ds(j * blocks_per_tile + u, 1), :], (SLC_BLOCK, Q_BLOCK))
             for u in range(blocks_per_tile)], axis=0)
        if diagonal:
            kpos = j * tk + lax.broadcasted_iota(I32, (tk, Q_BLOCK), 0)
            bias = jnp.where(kpos <= lane_qpos(tk), bias, NEG_INF)
        s = _dot_tb(ks_ref[0, pl.ds(r0, tk), :], qs)
        return _softmax_tile_t(s, bias, vs_ref[0, pl.ds(r0, tk), :], carry, nh)

    carry = lax.fori_loop(0, nkv - 1, lambda j, c: sel_tile(j, c, False), _softmax_init_t(nh))
    _, l_s, acc_s = sel_tile(nkv - 1, carry, True)
    o_sel = acc_s / jnp.maximum(l_s, 1e-30)

    span = WINDOW + Q_BLOCK
    start = pl.multiple_of(jnp.maximum(t0 - WINDOW, 0), Q_BLOCK)
    dist = lane_qpos(span) - (start + lax.broadcasted_iota(I32, (span, Q_BLOCK), 0))
    bias_w = jnp.where((dist >= 0) & (dist < WINDOW), 0.0, NEG_INF)
    s_w = _dot_tb(kw_ref[0, pl.ds(start, span), :], qs)
    _, l_w, acc_w = _softmax_tile_t(s_w, bias_w, vw_ref[0, pl.ds(start, span), :],
                                    _softmax_init_t(nh), nh)
    o_win = acc_w / jnp.maximum(l_w, 1e-30)

    for h in range(nh):
        sl = slice(h * Q_BLOCK, (h + 1) * Q_BLOCK)
        row = 3 * (g * nh + h)
        o = (gt_ref[pl.ds(row, 1), :] * o_cmp[:, sl]
             + gt_ref[pl.ds(row + 1, 1), :] * o_sel[:, sl]
             + gt_ref[pl.ds(row + 2, 1), :] * o_win[:, sl])
        o_ref[0, :, h * HEAD_DIM:(h + 1) * HEAD_DIM] = o.T.astype(o_ref.dtype)


def nsa_attention(rp3, projb3, proj3, kc, vc, gate_bias_row, slt, *, tk=512):
    B, T, _ = proj3.shape
    tk = min(tk, T)
    n_slc = T // SLC_BLOCK
    ksel = min(SLC_TOPK, n_slc)
    gw = C_GROUP_HEADS * HEAD_DIM
    assert T >= WINDOW + Q_BLOCK and T % tk == 0 and tk % SLC_BLOCK == 0

    def seq(col):
        return pl.BlockSpec((1, T, HEAD_DIM), lambda b, g, i, col=col: (b, 0, col + g))

    cmp_spec = pl.BlockSpec((1, 1, CMP_PER_SLC * n_slc, HEAD_DIM), lambda b, g, i: (b, g, 0, 0))
    return pl.pallas_call(
        functools.partial(_nsa_kernel, tk=tk, n_slc=n_slc, ksel=ksel),
        grid=(B, C_GROUPS, T // Q_BLOCK),
        in_specs=[pl.BlockSpec((1, Q_BLOCK, gw), lambda b, g, i: (b, i, g)),
                  cmp_spec, cmp_spec,
                  seq(PB["c_ks"]), seq(PB["c_vs"]), seq(PB["c_kw"]), seq(PB["c_vw"]),
                  pl.BlockSpec((1, Q_BLOCK, LANES), lambda b, g, i: (b, i, PB["c_g"])),
                  pl.BlockSpec((1, LANES), lambda b, g, i: (0, 0)),
                  pl.BlockSpec((n_slc, n_slc), lambda b, g, i: (0, 0))],
        out_specs=pl.BlockSpec((1, Q_BLOCK, gw), lambda b, g, i: (b, i, g)),
        out_shape=jax.ShapeDtypeStruct((B, T, C_W), BF16),
        scratch_shapes=[pltpu.VMEM((C_GROUP_HEADS * Q_BLOCK, HEAD_DIM), BF16),
                        pltpu.VMEM((LANES, Q_BLOCK), F32),
                        pltpu.VMEM((1, n_slc, Q_BLOCK), I32),
                        pltpu.VMEM((n_slc, Q_BLOCK), F32)],
        compiler_params=_cparams(("parallel", "parallel", "arbitrary")),
        name="nsa_attention",
    )(rp3, kc, vc, rp3, projb3, rp3, projb3, proj3, gate_bias_row, slt)


def _pad_cols(w, width):
    return jnp.pad(w, [(0, 0)] * (w.ndim - 1) + [(0, width - w.shape[-1])])


def _permute_w_in(w_in):
    offs = np.concatenate([[0], np.cumsum(IN_WIDTHS)])
    part = {n: w_in[..., offs[k]:offs[k + 1]] for k, n in enumerate(IN_NAMES)}
    for name in ("a_q", "c_q"):
        part[name] = part[name] * QK_FOLD
    zeros64 = jnp.zeros(w_in.shape[:-1] + (IDX_DIM,), w_in.dtype)
    order = sorted(PB, key=PB.get)
    cols = []
    for name, nxt in zip(order, order[1:] + [None]):
        end = (PB[nxt] if nxt else P_BLOCKS) * LANES
        if name == "ik_lo":
            blk = jnp.concatenate([part["b_ik"], zeros64], axis=-1)
        elif name == "ik_hi":
            blk = jnp.concatenate([zeros64, part["b_ik"]], axis=-1)
        else:
            blk = part[name]
        cols.append(_pad_cols(blk, end - PB[name] * LANES))
    return jnp.concatenate(cols, axis=-1).astype(BF16)


def _lane_row(v):
    return _pad_cols(v.reshape(1, -1).astype(F32), LANES)


def kernel(x, positions, g_mix_pre, g_mix_post, g_ffn_pre, g_ffn_post, w_in, fox_forget_bias,
           dsa_q_norm, w_dsa_uq, w_dsa_iq, nsa_cmp_pos_k, nsa_cmp_pos_v, w_nsa_cmp_k,
           w_nsa_cmp_v, nsa_gate_bias, w_gate_up, b_gate, w_branch, w_out, w_ffn_gate,
           w_ffn_up, w_ffn_down):
    B, T, D = x.shape
    depth = w_in.shape[0]
    M = B * T
    topk_b = min(IDX_TOPK_MAX, T // 4)
    n_slc = T // SLC_BLOCK
    tk = min(512, T)

    tab128, half128 = rope_tables(positions, HEAD_DIM)
    tab64, half64 = rope_tables(positions, IDX_DIM)
    tables = (tab128, tab64)
    proj_plan = [(0, half128)] * PB["ik_lo"] + [(1, half64)] * (ROPE_BLOCKS - PB["ik_lo"])
    q_plan = [(1, half64)] * QB["q"] + [(0, half128)] * B_HEADS
    slt_tk = strict_lower(tk)
    slt_blk = strict_lower(n_slc)
    d_ff = w_ffn_gate.shape[-1]
    ff_tn = next(t for t in (512, 256, 128) if d_ff % t == 0)
    ff_tk = d_ff // 2 if (d_ff // 2) % LANES == 0 else d_ff

    w_in_b = _permute_w_in(w_in)
    w_q_b = jnp.concatenate([w_dsa_iq, w_dsa_uq * QK_FOLD], axis=-1).astype(BF16)
    w_ck_b = w_nsa_cmp_k.reshape(depth, CMP_BLOCK, HEAD_DIM, HEAD_DIM).astype(BF16)
    w_cv_b = w_nsa_cmp_v.reshape(depth, CMP_BLOCK, HEAD_DIM, HEAD_DIM).astype(BF16)
    w_gate_b = w_gate_up.astype(BF16)
    b_gate3 = b_gate.reshape(depth, 1, -1)
    w_branch_b = w_branch.astype(BF16)
    w_out_b = w_out.astype(BF16)
    w_fg_b = w_ffn_gate.astype(BF16)
    w_fu_b = w_ffn_up.astype(BF16)
    w_fd_b = w_ffn_down.astype(BF16)

    xf = x.reshape(M, D)
    h = rmsnorm(xf, g_mix_pre[0], width=D)
    for l in range(depth):
        proj, projb = matmul(h, w_in_b, l, (F32, BF16))
        proj3 = proj.reshape(B, T, -1)
        projb3 = projb.reshape(B, T, -1)
        rp3 = rope(proj, proj_plan, tables).reshape(B, T, -1)

        f, ft = fox_gate(proj3, _lane_row(fox_forget_bias[l]))
        o_a = fox_attention(projb3, f, ft)

        cqn = rmsnorm(proj, dsa_q_norm[l], width=B_Q_LORA, col_block=PB["b_cq"] * LANES // B_Q_LORA)
        (qiq,) = matmul(cqn, w_q_b, l, (F32,))
        rqiq3 = rope(qiq, q_plan, tables).reshape(B, T, -1)
        o_b = dsa_attention(rqiq3, proj3, rp3, projb3, slt_tk, topk=topk_b, tk=tk)

        kc = nsa_compress(proj3, PB["c_kc"], nsa_cmp_pos_k, w_ck_b, l)
        vc = nsa_compress(proj3, PB["c_vc"], nsa_cmp_pos_v, w_cv_b, l)
        o_c = nsa_attention(rp3, projb3, proj3, kc, vc, _lane_row(nsa_gate_bias[l]),
                            slt_blk, tk=min(1024, T))

        y = merge(projb, o_a.reshape(M, -1), o_b.reshape(M, -1), o_c.reshape(M, -1),
                  w_gate_b, b_gate3, w_branch_b, l)
        (mix,) = matmul(y, w_out_b, l, (F32,))
        xf, h = resnorm(xf, mix, g_mix_post[l], g_ffn_pre[l])

        u = ffn_up(h, w_fg_b, w_fu_b, l, tn=ff_tn)
        ffn = matmul_ksplit(u, w_fd_b, l, F32, tm=1024, tn=512, tk=ff_tk)
        if l + 1 < depth:
            xf, h = resnorm(xf, ffn, g_ffn_post[l], g_mix_pre[l + 1])
        else:
            xf = resnorm(xf, ffn, g_ffn_post[l])
    return xf.reshape(B, T, D)
```

```python
import functools

import jax
import jax.numpy as jnp
import numpy as np
from jax import lax
from jax.experimental import pallas as pl
from jax.experimental.pallas import tpu as pltpu

F32 = jnp.float32
BF16 = jnp.bfloat16
I32 = jnp.int32

HEAD_DIM = 128
LANES = 128
ROPE_THETA = 500000.0
ROT_FRACTION = 4
NORM_EPS = 1e-6
NEG_INF = -1e30
INT_MIN = -(2 ** 31)

A_HEADS = 12
A_W = A_HEADS * HEAD_DIM
B_HEADS = 8
B_W = B_HEADS * HEAD_DIM
B_Q_LORA = 1024
IDX_HEADS = 32
IDX_DIM = 64
IDX_TOPK_MAX = 256
IDX_SCALE = (IDX_HEADS * IDX_DIM) ** -0.5
C_HEADS = 12
C_W = C_HEADS * HEAD_DIM
C_GROUPS = 2
C_GROUP_HEADS = C_HEADS // C_GROUPS
C_KV_W = C_GROUPS * HEAD_DIM
CMP_BLOCK = 32
CMP_STRIDE = 16
SLC_BLOCK = 64
SLC_TOPK = 16
CMP_PER_SLC = SLC_BLOCK // CMP_STRIDE
WINDOW = 512
FORCE_BONUS = 1e4
GATE_RANK = 512
Q_BLOCK = 128
LOG2E = 1.4426950408889634
QK_FOLD = HEAD_DIM ** -0.5 * LOG2E

IN_WIDTHS = (A_W, A_W, A_W, A_HEADS,
             B_Q_LORA, HEAD_DIM, HEAD_DIM, IDX_DIM, IDX_HEADS,
             C_W, C_KV_W, C_KV_W, C_KV_W, C_KV_W, C_KV_W, C_KV_W, 3 * C_HEADS,
             GATE_RANK)
IN_NAMES = ("a_q", "a_k", "a_v", "a_f", "b_cq", "b_k", "b_v", "b_ik", "b_iw",
            "c_q", "c_kc", "c_vc", "c_ks", "c_vs", "c_kw", "c_vw", "c_g", "z")

PB = dict(c_q=0, b_k=12, c_ks=13, c_kw=15, ik_lo=17, ik_hi=18, b_v=19, z=20, b_cq=24,
          a_q=32, a_k=44, a_v=56, c_kc=68, c_vc=70, c_vs=72, c_vw=74,
          a_f=76, b_iw=77, c_g=78)
P_BLOCKS = 80
ROPE_BLOCKS = 19
QB = dict(iq=0, q=IDX_HEADS * IDX_DIM // LANES)

VMEM_LIMIT = 56 * 1024 * 1024


def _cparams(sem, vmem=VMEM_LIMIT):
    return pltpu.CompilerParams(dimension_semantics=sem, vmem_limit_bytes=vmem)


def _dot(a, b):
    return jnp.dot(a, b, preferred_element_type=F32)


def _dot_tb(a, b):
    return lax.dot_general(a, b, (((1,), (1,)), ((), ())), preferred_element_type=F32)


def _dot_ta(a, b):
    return lax.dot_general(a, b, (((0,), (0,)), ((), ())), preferred_element_type=F32)


def _rmsnorm_kernel(x_ref, g_ref, o_ref):
    x = x_ref[...].astype(F32)
    y = x * lax.rsqrt(jnp.mean(x * x, axis=-1, keepdims=True) + NORM_EPS)
    o_ref[...] = (y * g_ref[...]).astype(o_ref.dtype)


def rmsnorm(x, g, *, width, col_block=0, tm=512):
    M = x.shape[0]
    tm = min(tm, M)
    return pl.pallas_call(
        _rmsnorm_kernel,
        grid=(M // tm,),
        in_specs=[pl.BlockSpec((tm, width), lambda i: (i, col_block)),
                  pl.BlockSpec((1, width), lambda i: (0, 0))],
        out_specs=pl.BlockSpec((tm, width), lambda i: (i, 0)),
        out_shape=jax.ShapeDtypeStruct((M, width), BF16),
        compiler_params=_cparams(("parallel",)),
        name="rmsnorm",
    )(x, g.reshape(1, width).astype(F32))


def _resnorm_kernel(x_ref, m_ref, g1_ref, g2_ref, xo_ref, ho_ref):
    m = m_ref[...]
    y = m * lax.rsqrt(jnp.mean(m * m, axis=-1, keepdims=True) + NORM_EPS) * g1_ref[...]
    xn = x_ref[...] + y
    xo_ref[...] = xn
    h = xn * lax.rsqrt(jnp.mean(xn * xn, axis=-1, keepdims=True) + NORM_EPS) * g2_ref[...]
    ho_ref[...] = h.astype(ho_ref.dtype)


def _resnorm_last_kernel(x_ref, m_ref, g1_ref, xo_ref):
    m = m_ref[...]
    y = m * lax.rsqrt(jnp.mean(m * m, axis=-1, keepdims=True) + NORM_EPS) * g1_ref[...]
    xo_ref[...] = x_ref[...] + y


def resnorm(x, m, g1, g2=None, *, tm=256):
    M, D = x.shape
    tm = min(tm, M)
    row = pl.BlockSpec((tm, D), lambda i: (i, 0))
    gain = pl.BlockSpec((1, D), lambda i: (0, 0))
    if g2 is None:
        return pl.pallas_call(
            _resnorm_last_kernel, grid=(M // tm,),
            in_specs=[row, row, gain], out_specs=row,
            out_shape=jax.ShapeDtypeStruct((M, D), F32),
            compiler_params=_cparams(("parallel",)), name="resnorm_last",
        )(x, m, g1.reshape(1, D))
    return pl.pallas_call(
        _resnorm_kernel, grid=(M // tm,),
        in_specs=[row, row, gain, gain], out_specs=[row, row],
        out_shape=[jax.ShapeDtypeStruct((M, D), F32), jax.ShapeDtypeStruct((M, D), BF16)],
        compiler_params=_cparams(("parallel",)), name="resnorm",
    )(x, m, g1.reshape(1, D), g2.reshape(1, D))


def _mm_kernel(x_ref, w_ref, *o_refs):
    acc = _dot(x_ref[...], w_ref[...])
    for o_ref in o_refs:
        o_ref[...] = acc.astype(o_ref.dtype)


def _layer_cols(layer, K, tn):
    return pl.BlockSpec((None, K, tn), lambda i, j: (layer, 0, j))


def matmul(x, w, layer, out_dtypes, *, tm=1024, tn=512):
    M, K = x.shape
    N = w.shape[2]
    tm, tn = min(tm, M), min(tn, N)
    outs = pl.pallas_call(
        _mm_kernel,
        grid=(M // tm, N // tn),
        in_specs=[pl.BlockSpec((tm, K), lambda i, j: (i, 0)), _layer_cols(layer, K, tn)],
        out_specs=[pl.BlockSpec((tm, tn), lambda i, j: (i, j)) for _ in out_dtypes],
        out_shape=[jax.ShapeDtypeStruct((M, N), dt) for dt in out_dtypes],
        compiler_params=_cparams(("parallel", "arbitrary")),
        name="matmul",
    )(x, w)
    return outs


def _mm_ksplit_kernel(x_ref, w_ref, o_ref, acc_ref):
    k = pl.program_id(2)

    @pl.when(k == 0)
    def _():
        acc_ref[...] = jnp.zeros_like(acc_ref)

    acc_ref[...] += _dot(x_ref[...], w_ref[...])

    @pl.when(k == pl.num_programs(2) - 1)
    def _():
        o_ref[...] = acc_ref[...].astype(o_ref.dtype)


def matmul_ksplit(x, w, layer, out_dtype, *, tm, tn, tk):
    M, K = x.shape
    N = w.shape[2]
    tm, tn, tk = min(tm, M), min(tn, N), min(tk, K)
    return pl.pallas_call(
        _mm_ksplit_kernel,
        grid=(M // tm, N // tn, K // tk),
        in_specs=[pl.BlockSpec((tm, tk), lambda i, j, k: (i, k)),
                  pl.BlockSpec((None, tk, tn), lambda i, j, k: (layer, k, j))],
        out_specs=pl.BlockSpec((tm, tn), lambda i, j, k: (i, j)),
        out_shape=jax.ShapeDtypeStruct((M, N), out_dtype),
        scratch_shapes=[pltpu.VMEM((tm, tn), F32)],
        compiler_params=_cparams(("parallel", "arbitrary", "arbitrary")),
        name="matmul_ksplit",
    )(x, w)


def _ffn_up_kernel(h_ref, wg_ref, wu_ref, o_ref):
    h = h_ref[...]
    g = _dot(h, wg_ref[...])
    u = _dot(h, wu_ref[...])
    o_ref[...] = (g * jax.nn.sigmoid(g) * u).astype(o_ref.dtype)


def ffn_up(h, wg, wu, layer, *, tm=1024, tn=256):
    M, K = h.shape
    N = wg.shape[2]
    tm, tn = min(tm, M), min(tn, N)
    return pl.pallas_call(
        _ffn_up_kernel,
        grid=(M // tm, N // tn),
        in_specs=[pl.BlockSpec((tm, K), lambda i, j: (i, 0)),
                  _layer_cols(layer, K, tn), _layer_cols(layer, K, tn)],
        out_specs=pl.BlockSpec((tm, tn), lambda i, j: (i, j)),
        out_shape=jax.ShapeDtypeStruct((M, N), BF16),
        compiler_params=_cparams(("parallel", "arbitrary")),
        name="ffn_up",
    )(h, wg, wu)


def _merge_kernel(z_ref, oa_ref, ob_ref, oc_ref, wga_ref, wgb_ref, wgc_ref,
                  bga_ref, bgb_ref, bgc_ref, wbr_ref, y_ref):
    z = z_ref[...]

    def branch(o_ref, row0, wg_ref, bg_ref):
        gate = jax.nn.sigmoid(_dot(z, wg_ref[...]) + bg_ref[...])
        rows = o_ref.shape[1]
        return gate * _dot(o_ref[...], wbr_ref[row0:row0 + rows, :])

    y = (branch(oa_ref, 0, wga_ref, bga_ref)
         + branch(ob_ref, A_W, wgb_ref, bgb_ref)
         + branch(oc_ref, A_W + B_W, wgc_ref, bgc_ref))
    y_ref[...] = y.astype(y_ref.dtype)


def merge(projb, o_a, o_b, o_c, w_gate_up, b_gate, w_branch, layer, *, tm=1024, tn=512):
    M = projb.shape[0]
    KB, D = w_branch.shape[1:]
    tm, tn = min(tm, M), min(tn, D)
    nj = D // tn
    z_blk = PB["z"] * LANES // GATE_RANK

    def full_k(o):
        return pl.BlockSpec((tm, o.shape[1]), lambda i, j: (i, 0))

    def gate_w(b):
        return pl.BlockSpec((None, GATE_RANK, tn), lambda i, j, b=b: (layer, 0, b * nj + j))

    def gate_b(b):
        return pl.BlockSpec((None, 1, tn), lambda i, j, b=b: (layer, 0, b * nj + j))

    return pl.pallas_call(
        _merge_kernel,
        grid=(M // tm, nj),
        in_specs=[pl.BlockSpec((tm, GATE_RANK), lambda i, j: (i, z_blk)),
                  full_k(o_a), full_k(o_b), full_k(o_c),
                  gate_w(0), gate_w(1), gate_w(2), gate_b(0), gate_b(1), gate_b(2),
                  _layer_cols(layer, KB, tn)],
        out_specs=pl.BlockSpec((tm, tn), lambda i, j: (i, j)),
        out_shape=jax.ShapeDtypeStruct((M, D), BF16),
        compiler_params=_cparams(("parallel", "arbitrary")),
        name="merge",
    )(projb, o_a, o_b, o_c, w_gate_up, w_gate_up, w_gate_up, b_gate, b_gate, b_gate, w_branch)


def _rope_kernel(x_ref, *refs, plan):
    o_ref = refs[-1]
    tables = refs[:-1]
    for c, (ts, half) in enumerate(plan):
        sl = slice(c * LANES, (c + 1) * LANES)
        x = x_ref[:, sl]
        cos, sin_up, sin_dn = (tables[3 * ts + t][...] for t in range(3))
        up = pltpu.roll(x, LANES - half, axis=1)
        dn = pltpu.roll(x, half, axis=1)
        o_ref[:, sl] = (x * cos + up * sin_up + dn * sin_dn).astype(o_ref.dtype)


def rope(x, plan, table_sets, *, tm=512):
    M = x.shape[0]
    tm = min(tm, M)
    width = len(plan) * LANES
    tab = pl.BlockSpec((tm, LANES), lambda i: (i, 0))
    flat = [t for ts in table_sets for t in ts]
    return pl.pallas_call(
        functools.partial(_rope_kernel, plan=tuple(plan)),
        grid=(M // tm,),
        in_specs=[pl.BlockSpec((tm, width), lambda i: (i, 0))] + [tab] * len(flat),
        out_specs=pl.BlockSpec((tm, width), lambda i: (i, 0)),
        out_shape=jax.ShapeDtypeStruct((M, width), BF16),
        compiler_params=_cparams(("parallel",)),
        name="rope",
    )(x, *flat)


def rope_tables(positions, head_dim):
    B, T = positions.shape
    rot = head_dim // ROT_FRACTION
    half = rot // 2
    inv_freq = ROPE_THETA ** (-jnp.arange(half, dtype=F32) / half)
    ang = positions.astype(F32)[:, :, None] * inv_freq
    cos, sin = jnp.cos(ang), jnp.sin(ang)
    rest = head_dim - rot
    zh = jnp.zeros((B, T, half), F32)
    c = jnp.concatenate([cos, cos, jnp.ones((B, T, rest), F32)], axis=-1)
    s1 = jnp.concatenate([-sin, zh, jnp.zeros((B, T, rest), F32)], axis=-1)
    s2 = jnp.concatenate([zh, sin, jnp.zeros((B, T, rest), F32)], axis=-1)
    reps = LANES // head_dim
    return tuple(jnp.tile(t, (1, 1, reps)).reshape(B * T, LANES) for t in (c, s1, s2)), half


def _fox_gate_kernel(a_ref, b_ref, f_ref, ft_ref, carry_ref, *, tb):
    @pl.when(pl.program_id(1) == 0)
    def _():
        carry_ref[...] = jnp.zeros_like(carry_ref)

    x = a_ref[0] + b_ref[...]
    log_f = jnp.minimum(x, 0.0) - jnp.log1p(jnp.exp(-jnp.abs(x)))
    row = lax.broadcasted_iota(I32, (tb, tb), 0)
    col = lax.broadcasted_iota(I32, (tb, tb), 1)
    tri = (col <= row).astype(F32)
    cum = jnp.dot(tri, log_f, preferred_element_type=F32,
                  precision=lax.Precision.HIGHEST) + carry_ref[...]
    cum2 = cum * LOG2E
    f_ref[0] = cum2
    ft_ref[0] = cum2.T[:16, :]
    carry_ref[...] = cum[tb - 1:tb, :]


def fox_gate(proj3, bias_row, *, tb=256):
    B, T, _ = proj3.shape
    tb = min(tb, T)
    return pl.pallas_call(
        functools.partial(_fox_gate_kernel, tb=tb),
        grid=(B, T // tb),
        in_specs=[pl.BlockSpec((1, tb, LANES), lambda b, j: (b, j, PB["a_f"])),
                  pl.BlockSpec((1, LANES), lambda b, j: (0, 0))],
        out_specs=[pl.BlockSpec((1, tb, LANES), lambda b, j: (b, j, 0)),
                   pl.BlockSpec((1, 16, tb), lambda b, j: (b, 0, j))],
        out_shape=[jax.ShapeDtypeStruct((B, T, LANES), F32),
                   jax.ShapeDtypeStruct((B, 16, T), F32)],
        scratch_shapes=[pltpu.VMEM((1, LANES), F32)],
        compiler_params=_cparams(("parallel", "arbitrary")),
        name="fox_gate",
    )(proj3, bias_row)


FOX_HEADS_PER_STEP = 1


def _fox_kernel(q_ref, k_ref, v_ref, f_ref, ft_ref, o_ref, *, tq, tk):
    nh = FOX_HEADS_PER_STEP
    hg = pl.program_id(1)
    i = pl.program_id(2)
    lane = lax.broadcasted_iota(I32, (tq, LANES), 1)
    f_rows = f_ref[0]
    heads = [slice(g * HEAD_DIM, (g + 1) * HEAD_DIM) for g in range(nh)]
    q = [q_ref[0, :, sl] for sl in heads]
    fq = [jnp.sum(jnp.where(lane == hg * nh + g, f_rows, 0.0), axis=1, keepdims=True)
          for g in range(nh)]
    nkv = (i * tq + tq + tk - 1) // tk

    def update(j, carry, diagonal):
        r0 = pl.multiple_of(j * tk, tk)
        out = []
        for g in range(nh):
            m, l, acc = carry[g]
            fk = ft_ref[0, g, pl.ds(j, 1), :]
            t = _dot_tb(q[g], k_ref[0, pl.ds(r0, tk), heads[g]]) - fk
            if diagonal:
                qpos = i * tq + lax.broadcasted_iota(I32, (tq, tk), 0)
                kpos = j * tk + lax.broadcasted_iota(I32, (tq, tk), 1)
                t = jnp.where(kpos <= qpos, t, NEG_INF)
            m_new = jnp.maximum(m, jnp.max(t, axis=1, keepdims=True) + fq[g])
            p = jnp.exp2(t + (fq[g] - m_new))
            alpha = jnp.exp2(m - m_new)
            l = alpha * l + jnp.sum(p, axis=1, keepdims=True)
            acc = alpha * acc + _dot(p.astype(BF16), v_ref[0, pl.ds(r0, tk), heads[g]])
            out.append((m_new, l, acc))
        return tuple(out)

    init = tuple((jnp.full((tq, 1), NEG_INF, F32), jnp.zeros((tq, 1), F32),
                  jnp.zeros((tq, HEAD_DIM), F32)) for _ in range(nh))
    carry = lax.fori_loop(0, nkv - 1, lambda j, c: update(j, c, False), init)
    carry = update(nkv - 1, carry, True)
    for g in range(nh):
        _, l, acc = carry[g]
        o_ref[0, :, heads[g]] = (acc / jnp.maximum(l, 1e-30)).astype(o_ref.dtype)


def fox_attention(projb3, f, ft, *, tq=1024, tk=1024):
    B, T, _ = projb3.shape
    tq, tk = min(tq, T), min(tk, T)
    nh = FOX_HEADS_PER_STEP
    gw = nh * HEAD_DIM
    assert tk % tq == 0 and A_HEADS % nh == 0
    assert PB["a_q"] % nh == 0 and PB["a_k"] % nh == 0 and PB["a_v"] % nh == 0
    ft4 = ft.reshape(B, 16, T // tk, tk)
    return pl.pallas_call(
        functools.partial(_fox_kernel, tq=tq, tk=tk),
        grid=(B, A_HEADS // nh, T // tq),
        in_specs=[pl.BlockSpec((1, tq, gw), lambda b, h, i: (b, i, PB["a_q"] // nh + h)),
                  pl.BlockSpec((1, T, gw), lambda b, h, i: (b, 0, PB["a_k"] // nh + h)),
                  pl.BlockSpec((1, T, gw), lambda b, h, i: (b, 0, PB["a_v"] // nh + h)),
                  pl.BlockSpec((1, tq, LANES), lambda b, h, i: (b, i, 0)),
                  pl.BlockSpec((1, nh, T // tk, tk), lambda b, h, i: (b, h, 0, 0))],
        out_specs=pl.BlockSpec((1, tq, gw), lambda b, h, i: (b, i, h)),
        out_shape=jax.ShapeDtypeStruct((B, T, A_W), BF16),
        compiler_params=_cparams(("parallel", "parallel", "arbitrary")),
        name="fox_attention",
    )(projb3, projb3, projb3, f, ft4)


def _sortable_key(x):
    b = pltpu.bitcast(x, I32)
    return jnp.where(b >= 0, b, b ^ jnp.int32(0x7FFFFFFF))


def _count_ge(key_ref, ntiles, cand, rows):
    def tile(j, cnt):
        ge = jnp.where(key_ref[j] >= cand, 1, 0).astype(I32)
        return cnt + jnp.sum(ge.reshape(rows // 8, 8, LANES), axis=0)

    cnt = lax.fori_loop(0, ntiles, tile, jnp.zeros((8, LANES), I32))
    return jnp.sum(cnt, axis=0, keepdims=True)


def _kth_largest(key_ref, ntiles, k, rows):
    nonneg = _count_ge(key_ref, ntiles, jnp.zeros((1, LANES), I32), rows)
    base0 = jnp.where(nonneg >= k, 0, INT_MIN).astype(I32)

    def step(s, base):
        cand = base | lax.shift_left(jnp.int32(1), 30 - s)
        cnt = _count_ge(key_ref, ntiles, cand, rows)
        return jnp.where(cnt >= k, cand, base)

    return lax.fori_loop(0, 31, step, base0)


def _topk_need(key_ref, ntiles, thr, k, rows):
    gt = _count_ge(key_ref, ntiles, thr + 1, rows)
    gt = jnp.where(thr == jnp.int32(2 ** 31 - 1), 0, gt)
    return (k - gt).astype(F32)


def _topk_select(key, thr, need, eq_before, slt):
    eq = key == thr
    eq_f = jnp.where(eq, 1.0, 0.0)
    rank = _dot(slt, eq_f.astype(BF16)) + eq_before
    sel = (key > thr) | (eq & (rank < need))
    return sel, eq_before + jnp.sum(eq_f, axis=0, keepdims=True)


def strict_lower(n):
    r = jnp.arange(n)
    return (r[None, :] < r[:, None]).astype(BF16)


def _softmax_tile_t(s, bias, v, carry, nheads):
    m, l, acc = carry
    ms, ls, ps = [], [], []
    for h in range(nheads):
        sl = slice(h * LANES, (h + 1) * LANES)
        sh = s[:, sl] + bias
        mh = jnp.maximum(m[:, sl], jnp.max(sh, axis=0, keepdims=True))
        ph = jnp.exp2(sh - mh)
        ms.append(mh)
        ls.append(jnp.sum(ph, axis=0, keepdims=True))
        ps.append(ph.astype(BF16))
    m_new = jnp.concatenate(ms, axis=1)
    alpha = jnp.exp2(m - m_new)
    l = alpha * l + jnp.concatenate(ls, axis=1)
    acc = alpha * acc + _dot_ta(v, jnp.concatenate(ps, axis=1))
    return m_new, l, acc


def _softmax_init_t(nheads):
    width = nheads * LANES
    return (jnp.full((1, width), NEG_INF, F32), jnp.zeros((1, width), F32),
            jnp.zeros((HEAD_DIM, width), F32))


def _dsa_kernel(iq_ref, iw_ref, q_ref, ika_ref, ikb_ref, k_ref, v_ref, slt_ref, o_ref,
                iqs_ref, qs_ref, wt_ref, key_ref, *, tk, topk):
    i = pl.program_id(1)
    t0 = i * Q_BLOCK
    nkv = (t0 + Q_BLOCK + tk - 1) // tk
    npair = IDX_HEADS // 2
    for p in range(npair):
        iqs_ref[p * Q_BLOCK:(p + 1) * Q_BLOCK, :] = iq_ref[0, :, p * LANES:(p + 1) * LANES]
    for h in range(B_HEADS):
        qs_ref[h * Q_BLOCK:(h + 1) * Q_BLOCK, :] = q_ref[0, :, h * HEAD_DIM:(h + 1) * HEAD_DIM]
    wt_ref[...] = (iw_ref[0] * IDX_SCALE).T
    qpos = t0 + lax.broadcasted_iota(I32, (tk, LANES), 1)
    krow = lax.broadcasted_iota(I32, (tk, LANES), 0)
    chunk = 4

    def score_tile(j, _):
        r0 = pl.multiple_of(j * tk, tk)
        ika = ika_ref[0, pl.ds(r0, tk), :]
        ikb = ikb_ref[0, pl.ds(r0, tk), :]
        acc = jnp.zeros((tk, LANES), F32)
        for c in range(npair // chunk):
            iqc = iqs_ref[c * chunk * Q_BLOCK:(c + 1) * chunk * Q_BLOCK, :]
            ra = _dot_tb(ika, iqc)
            rb = _dot_tb(ikb, iqc)
            for u in range(chunk):
                p = c * chunk + u
                sl = slice(u * LANES, (u + 1) * LANES)
                acc = acc + jnp.maximum(ra[:, sl], 0.0) * wt_ref[2 * p:2 * p + 1, :]
                acc = acc + jnp.maximum(rb[:, sl], 0.0) * wt_ref[2 * p + 1:2 * p + 2, :]
        acc = jnp.where((j * tk + krow) <= qpos, acc, NEG_INF)
        key_ref[j] = _sortable_key(acc)
        return 0

    lax.fori_loop(0, nkv, score_tile, 0)
    thr = _kth_largest(key_ref, nkv, topk, tk)
    need = _topk_need(key_ref, nkv, thr, topk, tk)
    qs = qs_ref[...]
    slt = slt_ref[...]

    def attn_tile(j, carry, diagonal):
        eq_before = carry[3]
        r0 = pl.multiple_of(j * tk, tk)
        sel, eq_before = _topk_select(key_ref[j], thr, need, eq_before, slt)
        if diagonal:
            sel = sel & ((j * tk + krow) <= qpos)
        bias = jnp.where(sel, 0.0, NEG_INF)
        s = _dot_tb(k_ref[0, pl.ds(r0, tk), :], qs)
        return _softmax_tile_t(s, bias, v_ref[0, pl.ds(r0, tk), :], carry[:3], B_HEADS) + (eq_before,)

    init = _softmax_init_t(B_HEADS) + (jnp.zeros((1, LANES), F32),)
    carry = lax.fori_loop(0, nkv - 1, lambda j, c: attn_tile(j, c, False), init)
    _, l, acc, _ = attn_tile(nkv - 1, carry, True)
    out = acc / jnp.maximum(l, 1e-30)
    for h in range(B_HEADS):
        o_ref[0, :, h * HEAD_DIM:(h + 1) * HEAD_DIM] = (
            out[:, h * Q_BLOCK:(h + 1) * Q_BLOCK].T.astype(o_ref.dtype))


def dsa_attention(rqiq3, proj3, rp3, projb3, slt, *, topk, tk=512):
    B, T, _ = proj3.shape
    tk = min(tk, T)
    assert tk >= topk and T % tk == 0
    return pl.pallas_call(
        functools.partial(_dsa_kernel, tk=tk, topk=topk),
        grid=(B, T // Q_BLOCK),
        in_specs=[pl.BlockSpec((1, Q_BLOCK, IDX_HEADS * IDX_DIM), lambda b, i: (b, i, QB["iq"])),
                  pl.BlockSpec((1, Q_BLOCK, LANES), lambda b, i: (b, i, PB["b_iw"])),
                  pl.BlockSpec((1, Q_BLOCK, B_W), lambda b, i: (b, i, QB["q"] * LANES // B_W)),
                  pl.BlockSpec((1, T, LANES), lambda b, i: (b, 0, PB["ik_lo"])),
                  pl.BlockSpec((1, T, LANES), lambda b, i: (b, 0, PB["ik_hi"])),
                  pl.BlockSpec((1, T, HEAD_DIM), lambda b, i: (b, 0, PB["b_k"])),
                  pl.BlockSpec((1, T, HEAD_DIM), lambda b, i: (b, 0, PB["b_v"])),
                  pl.BlockSpec((tk, tk), lambda b, i: (0, 0))],
        out_specs=pl.BlockSpec((1, Q_BLOCK, B_W), lambda b, i: (b, i, 0)),
        out_shape=jax.ShapeDtypeStruct((B, T, B_W), BF16),
        scratch_shapes=[pltpu.VMEM((IDX_HEADS // 2 * Q_BLOCK, LANES), BF16),
                        pltpu.VMEM((B_HEADS * Q_BLOCK, HEAD_DIM), BF16),
                        pltpu.VMEM((LANES, Q_BLOCK), F32),
                        pltpu.VMEM((T // tk, tk, LANES), I32)],
        compiler_params=_cparams(("parallel", "arbitrary")),
        name="dsa_attention",
    )(rqiq3, proj3, rqiq3, rp3, rp3, rp3, projb3, slt)


def _compress_kernel(x_ref, pe_ref, w_ref, o_ref, *, n_slc):
    def window_sum(r, lo, hi):
        acc = jnp.zeros((n_slc, HEAD_DIM), F32)
        for l in range(lo, hi):
            off = (CMP_STRIDE * r + l) % SLC_BLOCK
            rows = x_ref[0, pl.ds(off, n_slc, stride=SLC_BLOCK), :] + pe_ref[l:l + 1, :]
            acc = acc + _dot(rows.astype(BF16), w_ref[l])
        return acc

    for r in range(CMP_PER_SLC):
        inside = min(CMP_BLOCK, SLC_BLOCK - CMP_STRIDE * r)
        acc = window_sum(r, 0, inside)
        if inside < CMP_BLOCK:
            acc = acc + pltpu.roll(window_sum(r, inside, CMP_BLOCK), n_slc - 1, axis=0)
        o_ref[0, 0, r * n_slc:(r + 1) * n_slc, :] = acc.astype(o_ref.dtype)


def nsa_compress(proj3, col_block, pe, w4, layer):
    B, T, _ = proj3.shape
    n_slc = T // SLC_BLOCK
    return pl.pallas_call(
        functools.partial(_compress_kernel, n_slc=n_slc),
        grid=(B, C_GROUPS),
        in_specs=[pl.BlockSpec((1, T, HEAD_DIM), lambda b, g: (b, 0, col_block + g)),
                  pl.BlockSpec((None, CMP_BLOCK, HEAD_DIM), lambda b, g: (layer, 0, 0)),
                  pl.BlockSpec((None, CMP_BLOCK, HEAD_DIM, HEAD_DIM), lambda b, g: (layer, 0, 0, 0))],
        out_specs=pl.BlockSpec((1, 1, CMP_PER_SLC * n_slc, HEAD_DIM), lambda b, g: (b, g, 0, 0)),
        out_shape=jax.ShapeDtypeStruct((B, C_GROUPS, CMP_PER_SLC * n_slc, HEAD_DIM), BF16),
        compiler_params=_cparams(("parallel", "parallel")),
        name="nsa_compress",
    )(proj3, pe, w4)


def _nsa_kernel(q_ref, kc_ref, vc_ref, ks_ref, vs_ref, kw_ref, vw_ref, g_ref, gb_ref,
                slt_ref, o_ref, qs_ref, gt_ref, key_ref, sb_ref, *, tk, n_slc, ksel):
    g = pl.program_id(1)
    i = pl.program_id(2)
    t0 = i * Q_BLOCK
    nh = C_GROUP_HEADS
    for h in range(nh):
        qs_ref[h * Q_BLOCK:(h + 1) * Q_BLOCK, :] = q_ref[0, :, h * HEAD_DIM:(h + 1) * HEAD_DIM]
    qs = qs_ref[...]
    gt_ref[...] = jax.nn.sigmoid(g_ref[0] + gb_ref[...]).T

    def lane_qpos(rows):
        return t0 + lax.broadcasted_iota(I32, (rows, Q_BLOCK), 1)

    nc = CMP_PER_SLC * n_slc
    crow = lax.broadcasted_iota(I32, (nc, Q_BLOCK), 0)
    cend = (crow % n_slc) * SLC_BLOCK + (crow // n_slc) * CMP_STRIDE + (CMP_BLOCK - 1)
    bias_c = jnp.where(cend <= lane_qpos(nc), 0.0, NEG_INF)
    has_cmp = lane_qpos(1) >= CMP_BLOCK - 1
    s_c = _dot_tb(kc_ref[0, 0], qs)
    ph = jnp.zeros((nc, Q_BLOCK), F32)
    p_heads = []
    for h in range(nh):
        sh = s_c[:, h * Q_BLOCK:(h + 1) * Q_BLOCK] + bias_c
        e = jnp.exp2(sh - jnp.max(sh, axis=0, keepdims=True))
        den = jnp.maximum(jnp.sum(e, axis=0, keepdims=True), 1e-30)
        pn = e * jnp.where(has_cmp, 1.0 / den, 0.0)
        ph = ph + pn
        p_heads.append(pn.astype(BF16))
    o_cmp = _dot_ta(vc_ref[0, 0], jnp.concatenate(p_heads, axis=1))

    imp = ph[0:n_slc, :]
    for r in range(1, CMP_PER_SLC):
        imp = imp + ph[r * n_slc:(r + 1) * n_slc, :]
    blk = lax.broadcasted_iota(I32, (n_slc, Q_BLOCK), 0)
    qp = lane_qpos(n_slc)
    cur = qp // SLC_BLOCK
    forced = (blk == 0) | (blk == cur) | (blk == cur - 1)
    imp = jnp.where(blk * SLC_BLOCK <= qp, imp + jnp.where(forced, FORCE_BONUS, 0.0), NEG_INF)
    key_ref[0] = _sortable_key(imp)
    thr = _kth_largest(key_ref, 1, ksel, n_slc)
    need = _topk_need(key_ref, 1, thr, ksel, n_slc)
    sel, _ = _topk_select(key_ref[0], thr, need, jnp.zeros((1, Q_BLOCK), F32), slt_ref[...])
    sb_ref[...] = jnp.where(sel, 0.0, NEG_INF)

    nkv = (t0 + Q_BLOCK + tk - 1) // tk
    blocks_per_tile = tk // SLC_BLOCK

    def sel_tile(j, carry, diagonal):
        r0 = pl.multiple_of(j * tk, tk)
        bias = jnp.concatenate(
            [jnp.broadcast_to(sb_ref[pl.ds(j * blocks_per_tile + u, 1), :], (SLC_BLOCK, Q_BLOCK))
             for u in range(blocks_per_tile)], axis=0)
        if diagonal:
            kpos = j * tk + lax.broadcasted_iota(I32, (tk, Q_BLOCK), 0)
            bias = jnp.where(kpos <= lane_qpos(tk), bias, NEG_INF)
        s = _dot_tb(ks_ref[0, pl.ds(r0, tk), :], qs)
        return _softmax_tile_t(s, bias, vs_ref[0, pl.ds(r0, tk), :], carry, nh)

    carry = lax.fori_loop(0, nkv - 1, lambda j, c: sel_tile(j, c, False), _softmax_init_t(nh))
    _, l_s, acc_s = sel_tile(nkv - 1, carry, True)
    o_sel = acc_s / jnp.maximum(l_s, 1e-30)

    span = WINDOW + Q_BLOCK
    start = pl.multiple_of(jnp.maximum(t0 - WINDOW, 0), Q_BLOCK)
    dist = lane_qpos(span) - (start + lax.broadcasted_iota(I32, (span, Q_BLOCK), 0))
    bias_w = jnp.where((dist >= 0) & (dist < WINDOW), 0.0, NEG_INF)
    s_w = _dot_tb(kw_ref[0, pl.ds(start, span), :], qs)
    _, l_w, acc_w = _softmax_tile_t(s_w, bias_w, vw_ref[0, pl.ds(start, span), :],
                                    _softmax_init_t(nh), nh)
    o_win = acc_w / jnp.maximum(l_w, 1e-30)

    for h in range(nh):
        sl = slice(h * Q_BLOCK, (h + 1) * Q_BLOCK)
        row = 3 * (g * nh + h)
        o = (gt_ref[pl.ds(row, 1), :] * o_cmp[:, sl]
             + gt_ref[pl.ds(row + 1, 1), :] * o_sel[:, sl]
             + gt_ref[pl.ds(row + 2, 1), :] * o_win[:, sl])
        o_ref[0, :, h * HEAD_DIM:(h + 1) * HEAD_DIM] = o.T.astype(o_ref.dtype)


def nsa_attention(rp3, projb3, proj3, kc, vc, gate_bias_row, slt, *, tk=512):
    B, T, _ = proj3.shape
    tk = min(tk, T)
    n_slc = T // SLC_BLOCK
    ksel = min(SLC_TOPK, n_slc)
    gw = C_GROUP_HEADS * HEAD_DIM
    assert T >= WINDOW + Q_BLOCK and T % tk == 0 and tk % SLC_BLOCK == 0

    def seq(col):
        return pl.BlockSpec((1, T, HEAD_DIM), lambda b, g, i, col=col: (b, 0, col + g))

    cmp_spec = pl.BlockSpec((1, 1, CMP_PER_SLC * n_slc, HEAD_DIM), lambda b, g, i: (b, g, 0, 0))
    return pl.pallas_call(
        functools.partial(_nsa_kernel, tk=tk, n_slc=n_slc, ksel=ksel),
        grid=(B, C_GROUPS, T // Q_BLOCK),
        in_specs=[pl.BlockSpec((1, Q_BLOCK, gw), lambda b, g, i: (b, i, g)),
                  cmp_spec, cmp_spec,
                  seq(PB["c_ks"]), seq(PB["c_vs"]), seq(PB["c_kw"]), seq(PB["c_vw"]),
                  pl.BlockSpec((1, Q_BLOCK, LANES), lambda b, g, i: (b, i, PB["c_g"])),
                  pl.BlockSpec((1, LANES), lambda b, g, i: (0, 0)),
                  pl.BlockSpec((n_slc, n_slc), lambda b, g, i: (0, 0))],
        out_specs=pl.BlockSpec((1, Q_BLOCK, gw), lambda b, g, i: (b, i, g)),
        out_shape=jax.ShapeDtypeStruct((B, T, C_W), BF16),
        scratch_shapes=[pltpu.VMEM((C_GROUP_HEADS * Q_BLOCK, HEAD_DIM), BF16),
                        pltpu.VMEM((LANES, Q_BLOCK), F32),
                        pltpu.VMEM((1, n_slc, Q_BLOCK), I32),
                        pltpu.VMEM((n_slc, Q_BLOCK), F32)],
        compiler_params=_cparams(("parallel", "parallel", "arbitrary")),
        name="nsa_attention",
    )(rp3, kc, vc, rp3, projb3, rp3, projb3, proj3, gate_bias_row, slt)


def _pad_cols(w, width):
    return jnp.pad(w, [(0, 0)] * (w.ndim - 1) + [(0, width - w.shape[-1])])


def _permute_w_in(w_in):
    offs = np.concatenate([[0], np.cumsum(IN_WIDTHS)])
    part = {n: w_in[..., offs[k]:offs[k + 1]] for k, n in enumerate(IN_NAMES)}
    for name in ("a_q", "c_q"):
        part[name] = part[name] * QK_FOLD
    part = {n: p.astype(BF16) for n, p in part.items()}
    zeros64 = jnp.zeros(w_in.shape[:-1] + (IDX_DIM,), BF16)
    order = sorted(PB, key=PB.get)
    cols = []
    for name, nxt in zip(order, order[1:] + [None]):
        end = (PB[nxt] if nxt else P_BLOCKS) * LANES
        if name == "ik_lo":
            blk = jnp.concatenate([part["b_ik"], zeros64], axis=-1)
        elif name == "ik_hi":
            blk = jnp.concatenate([zeros64, part["b_ik"]], axis=-1)
        else:
            blk = part[name]
        cols.append(_pad_cols(blk, end - PB[name] * LANES))
    return jnp.concatenate(cols, axis=-1)


def _lane_row(v):
    return _pad_cols(v.reshape(1, -1).astype(F32), LANES)


def kernel(x, positions, g_mix_pre, g_mix_post, g_ffn_pre, g_ffn_post, w_in, fox_forget_bias,
           dsa_q_norm, w_dsa_uq, w_dsa_iq, nsa_cmp_pos_k, nsa_cmp_pos_v, w_nsa_cmp_k,
           w_nsa_cmp_v, nsa_gate_bias, w_gate_up, b_gate, w_branch, w_out, w_ffn_gate,
           w_ffn_up, w_ffn_down):
    B, T, D = x.shape
    depth = w_in.shape[0]
    M = B * T
    topk_b = min(IDX_TOPK_MAX, T // 4)
    n_slc = T // SLC_BLOCK
    tk = min(512, T)

    tab128, half128 = rope_tables(positions, HEAD_DIM)
    tab64, half64 = rope_tables(positions, IDX_DIM)
    tables = (tab128, tab64)
    proj_plan = [(0, half128)] * PB["ik_lo"] + [(1, half64)] * (ROPE_BLOCKS - PB["ik_lo"])
    q_plan = [(1, half64)] * QB["q"] + [(0, half128)] * B_HEADS
    slt_tk = strict_lower(tk)
    slt_blk = strict_lower(n_slc)
    d_ff = w_ffn_gate.shape[-1]
    ff_tn = next(t for t in (512, 256, 128) if d_ff % t == 0)
    ff_tk = d_ff // 2 if (d_ff // 2) % LANES == 0 else d_ff

    w_in_b = _permute_w_in(w_in)
    w_q_b = jnp.concatenate([w_dsa_iq, w_dsa_uq * QK_FOLD], axis=-1).astype(BF16)
    w_ck_b = w_nsa_cmp_k.reshape(depth, CMP_BLOCK, HEAD_DIM, HEAD_DIM).astype(BF16)
    w_cv_b = w_nsa_cmp_v.reshape(depth, CMP_BLOCK, HEAD_DIM, HEAD_DIM).astype(BF16)
    w_gate_b = w_gate_up.astype(BF16)
    b_gate3 = b_gate.reshape(depth, 1, -1)
    w_branch_b = w_branch.astype(BF16)
    w_out_b = w_out.astype(BF16)
    w_fg_b = w_ffn_gate.astype(BF16)
    w_fu_b = w_ffn_up.astype(BF16)
    w_fd_b = w_ffn_down.astype(BF16)

    xf = x.reshape(M, D)
    h = rmsnorm(xf, g_mix_pre[0], width=D)
    for l in range(depth):
        proj, projb = matmul(h, w_in_b, l, (F32, BF16))
        proj3 = proj.reshape(B, T, -1)
        projb3 = projb.reshape(B, T, -1)
        rp3 = rope(proj, proj_plan, tables).reshape(B, T, -1)

        f, ft = fox_gate(proj3, _lane_row(fox_forget_bias[l]))
        o_a = fox_attention(projb3, f, ft)

        cqn = rmsnorm(proj, dsa_q_norm[l], width=B_Q_LORA, col_block=PB["b_cq"] * LANES // B_Q_LORA)
        (qiq,) = matmul(cqn, w_q_b, l, (F32,))
        rqiq3 = rope(qiq, q_plan, tables).reshape(B, T, -1)
        o_b = dsa_attention(rqiq3, proj3, rp3, projb3, slt_tk, topk=topk_b, tk=tk)

        kc = nsa_compress(proj3, PB["c_kc"], nsa_cmp_pos_k, w_ck_b, l)
        vc = nsa_compress(proj3, PB["c_vc"], nsa_cmp_pos_v, w_cv_b, l)
        o_c = nsa_attention(rp3, projb3, proj3, kc, vc, _lane_row(nsa_gate_bias[l]),
                            slt_blk, tk=min(1024, T))

        y = merge(projb, o_a.reshape(M, -1), o_b.reshape(M, -1), o_c.reshape(M, -1),
                  w_gate_b, b_gate3, w_branch_b, l)
        (mix,) = matmul(y, w_out_b, l, (F32,))
        xf, h = resnorm(xf, mix, g_mix_post[l], g_ffn_pre[l])

        u = ffn_up(h, w_fg_b, w_fu_b, l, tn=ff_tn)
        ffn = matmul_ksplit(u, w_fd_b, l, F32, tm=1024, tn=512, tk=ff_tk)
        if l + 1 < depth:
            xf, h = resnorm(xf, ffn, g_ffn_post[l], g_mix_pre[l + 1])
        else:
            xf = resnorm(xf, ffn, g_ffn_post[l])
    return xf.reshape(B, T, D)
```

```python
import functools

import jax
import jax.numpy as jnp
import numpy as np
from jax import lax
from jax.experimental import pallas as pl
from jax.experimental.pallas import tpu as pltpu

F32 = jnp.float32
BF16 = jnp.bfloat16
I32 = jnp.int32

HEAD_DIM = 128
LANES = 128
ROPE_THETA = 500000.0
ROT_FRACTION = 4
NORM_EPS = 1e-6
NEG_INF = -1e30
INT_MIN = -(2 ** 31)

A_HEADS = 12
A_W = A_HEADS * HEAD_DIM
B_HEADS = 8
B_W = B_HEADS * HEAD_DIM
B_Q_LORA = 1024
IDX_HEADS = 32
IDX_DIM = 64
IDX_TOPK_MAX = 256
IDX_SCALE = (IDX_HEADS * IDX_DIM) ** -0.5
C_HEADS = 12
C_W = C_HEADS * HEAD_DIM
C_GROUPS = 2
C_GROUP_HEADS = C_HEADS // C_GROUPS
C_KV_W = C_GROUPS * HEAD_DIM
CMP_BLOCK = 32
CMP_STRIDE = 16
SLC_BLOCK = 64
SLC_TOPK = 16
CMP_PER_SLC = SLC_BLOCK // CMP_STRIDE
WINDOW = 512
FORCE_BONUS = 1e4
GATE_RANK = 512
Q_BLOCK = 128
LOG2E = 1.4426950408889634
QK_FOLD = HEAD_DIM ** -0.5 * LOG2E

IN_WIDTHS = (A_W, A_W, A_W, A_HEADS,
             B_Q_LORA, HEAD_DIM, HEAD_DIM, IDX_DIM, IDX_HEADS,
             C_W, C_KV_W, C_KV_W, C_KV_W, C_KV_W, C_KV_W, C_KV_W, 3 * C_HEADS,
             GATE_RANK)
IN_NAMES = ("a_q", "a_k", "a_v", "a_f", "b_cq", "b_k", "b_v", "b_ik", "b_iw",
            "c_q", "c_kc", "c_vc", "c_ks", "c_vs", "c_kw", "c_vw", "c_g", "z")

PB = dict(c_q=0, b_k=12, c_ks=13, c_kw=15, ik_lo=17, ik_hi=18, b_v=19, z=20, b_cq=24,
          a_q=32, a_k=44, a_v=56, c_kc=68, c_vc=70, c_vs=72, c_vw=74,
          a_f=76, b_iw=77, c_g=78)
P_BLOCKS = 80
ROPE_BLOCKS = 19
QB = dict(iq=0, q=IDX_HEADS * IDX_DIM // LANES)

VMEM_LIMIT = 56 * 1024 * 1024


def _cparams(sem, vmem=VMEM_LIMIT):
    return pltpu.CompilerParams(dimension_semantics=sem, vmem_limit_bytes=vmem)


def _dot(a, b):
    return jnp.dot(a, b, preferred_element_type=F32)


def _dot_tb(a, b):
    return lax.dot_general(a, b, (((1,), (1,)), ((), ())), preferred_element_type=F32)


def _dot_ta(a, b):
    return lax.dot_general(a, b, (((0,), (0,)), ((), ())), preferred_element_type=F32)


def _rmsnorm_kernel(x_ref, g_ref, o_ref):
    x = x_ref[...].astype(F32)
    y = x * lax.rsqrt(jnp.mean(x * x, axis=-1, keepdims=True) + NORM_EPS)
    o_ref[...] = (y * g_ref[...]).astype(o_ref.dtype)


def rmsnorm(x, g, *, width, col_block=0, tm=512):
    M = x.shape[0]
    tm = min(tm, M)
    return pl.pallas_call(
        _rmsnorm_kernel,
        grid=(M // tm,),
        in_specs=[pl.BlockSpec((tm, width), lambda i: (i, col_block)),
                  pl.BlockSpec((1, width), lambda i: (0, 0))],
        out_specs=pl.BlockSpec((tm, width), lambda i: (i, 0)),
        out_shape=jax.ShapeDtypeStruct((M, width), BF16),
        compiler_params=_cparams(("parallel",)),
        name="rmsnorm",
    )(x, g.reshape(1, width).astype(F32))


def _resnorm_kernel(x_ref, m_ref, g1_ref, g2_ref, xo_ref, ho_ref):
    m = m_ref[...]
    y = m * lax.rsqrt(jnp.mean(m * m, axis=-1, keepdims=True) + NORM_EPS) * g1_ref[...]
    xn = x_ref[...] + y
    xo_ref[...] = xn
    h = xn * lax.rsqrt(jnp.mean(xn * xn, axis=-1, keepdims=True) + NORM_EPS) * g2_ref[...]
    ho_ref[...] = h.astype(ho_ref.dtype)


def _resnorm_last_kernel(x_ref, m_ref, g1_ref, xo_ref):
    m = m_ref[...]
    y = m * lax.rsqrt(jnp.mean(m * m, axis=-1, keepdims=True) + NORM_EPS) * g1_ref[...]
    xo_ref[...] = x_ref[...] + y


def resnorm(x, m, g1, g2=None, *, tm=256):
    M, D = x.shape
    tm = min(tm, M)
    row = pl.BlockSpec((tm, D), lambda i: (i, 0))
    gain = pl.BlockSpec((1, D), lambda i: (0, 0))
    if g2 is None:
        return pl.pallas_call(
            _resnorm_last_kernel, grid=(M // tm,),
            in_specs=[row, row, gain], out_specs=row,
            out_shape=jax.ShapeDtypeStruct((M, D), F32),
            compiler_params=_cparams(("parallel",)), name="resnorm_last",
        )(x, m, g1.reshape(1, D))
    return pl.pallas_call(
        _resnorm_kernel, grid=(M // tm,),
        in_specs=[row, row, gain, gain], out_specs=[row, row],
        out_shape=[jax.ShapeDtypeStruct((M, D), F32), jax.ShapeDtypeStruct((M, D), BF16)],
        compiler_params=_cparams(("parallel",)), name="resnorm",
    )(x, m, g1.reshape(1, D), g2.reshape(1, D))


def _mm_kernel(x_ref, w_ref, *o_refs):
    acc = _dot(x_ref[...], w_ref[...])
    for o_ref in o_refs:
        o_ref[...] = acc.astype(o_ref.dtype)


def _layer_cols(layer, K, tn):
    return pl.BlockSpec((None, K, tn), lambda i, j: (layer, 0, j))


def matmul(x, w, layer, out_dtypes, *, tm=1024, tn=512):
    M, K = x.shape
    N = w.shape[2]
    tm, tn = min(tm, M), min(tn, N)
    outs = pl.pallas_call(
        _mm_kernel,
        grid=(M // tm, N // tn),
        in_specs=[pl.BlockSpec((tm, K), lambda i, j: (i, 0)), _layer_cols(layer, K, tn)],
        out_specs=[pl.BlockSpec((tm, tn), lambda i, j: (i, j)) for _ in out_dtypes],
        out_shape=[jax.ShapeDtypeStruct((M, N), dt) for dt in out_dtypes],
        compiler_params=_cparams(("parallel", "arbitrary")),
        name="matmul",
    )(x, w)
    return outs


def _mm_ksplit_kernel(x_ref, w_ref, o_ref, acc_ref):
    k = pl.program_id(2)

    @pl.when(k == 0)
    def _():
        acc_ref[...] = jnp.zeros_like(acc_ref)

    acc_ref[...] += _dot(x_ref[...], w_ref[...])

    @pl.when(k == pl.num_programs(2) - 1)
    def _():
        o_ref[...] = acc_ref[...].astype(o_ref.dtype)


def matmul_ksplit(x, w, layer, out_dtype, *, tm, tn, tk):
    M, K = x.shape
    N = w.shape[2]
    tm, tn, tk = min(tm, M), min(tn, N), min(tk, K)
    return pl.pallas_call(
        _mm_ksplit_kernel,
        grid=(M // tm, N // tn, K // tk),
        in_specs=[pl.BlockSpec((tm, tk), lambda i, j, k: (i, k)),
                  pl.BlockSpec((None, tk, tn), lambda i, j, k: (layer, k, j))],
        out_specs=pl.BlockSpec((tm, tn), lambda i, j, k: (i, j)),
        out_shape=jax.ShapeDtypeStruct((M, N), out_dtype),
        scratch_shapes=[pltpu.VMEM((tm, tn), F32)],
        compiler_params=_cparams(("parallel", "arbitrary", "arbitrary")),
        name="matmul_ksplit",
    )(x, w)


def _ffn_up_kernel(h_ref, wg_ref, wu_ref, o_ref):
    h = h_ref[...]
    g = _dot(h, wg_ref[...])
    u = _dot(h, wu_ref[...])
    o_ref[...] = (g * jax.nn.sigmoid(g) * u).astype(o_ref.dtype)


def ffn_up(h, wg, wu, layer, *, tm=1024, tn=256):
    M, K = h.shape
    N = wg.shape[2]
    tm, tn = min(tm, M), min(tn, N)
    return pl.pallas_call(
        _ffn_up_kernel,
        grid=(M // tm, N // tn),
        in_specs=[pl.BlockSpec((tm, K), lambda i, j: (i, 0)),
                  _layer_cols(layer, K, tn), _layer_cols(layer, K, tn)],
        out_specs=pl.BlockSpec((tm, tn), lambda i, j: (i, j)),
        out_shape=jax.ShapeDtypeStruct((M, N), BF16),
        compiler_params=_cparams(("parallel", "arbitrary")),
        name="ffn_up",
    )(h, wg, wu)


def _merge_kernel(z_ref, oa_ref, ob_ref, oc_ref, wga_ref, wgb_ref, wgc_ref,
                  bga_ref, bgb_ref, bgc_ref, wbr_ref, y_ref):
    z = z_ref[...]

    def branch(o_ref, row0, wg_ref, bg_ref):
        gate = jax.nn.sigmoid(_dot(z, wg_ref[...]) + bg_ref[...])
        rows = o_ref.shape[1]
        return gate * _dot(o_ref[...], wbr_ref[row0:row0 + rows, :])

    y = (branch(oa_ref, 0, wga_ref, bga_ref)
         + branch(ob_ref, A_W, wgb_ref, bgb_ref)
         + branch(oc_ref, A_W + B_W, wgc_ref, bgc_ref))
    y_ref[...] = y.astype(y_ref.dtype)


def merge(projb, o_a, o_b, o_c, w_gate_up, b_gate, w_branch, layer, *, tm=1024, tn=512):
    M = projb.shape[0]
    KB, D = w_branch.shape[1:]
    tm, tn = min(tm, M), min(tn, D)
    nj = D // tn
    z_blk = PB["z"] * LANES // GATE_RANK

    def full_k(o):
        return pl.BlockSpec((tm, o.shape[1]), lambda i, j: (i, 0))

    def gate_w(b):
        return pl.BlockSpec((None, GATE_RANK, tn), lambda i, j, b=b: (layer, 0, b * nj + j))

    def gate_b(b):
        return pl.BlockSpec((None, 1, tn), lambda i, j, b=b: (layer, 0, b * nj + j))

    return pl.pallas_call(
        _merge_kernel,
        grid=(M // tm, nj),
        in_specs=[pl.BlockSpec((tm, GATE_RANK), lambda i, j: (i, z_blk)),
                  full_k(o_a), full_k(o_b), full_k(o_c),
                  gate_w(0), gate_w(1), gate_w(2), gate_b(0), gate_b(1), gate_b(2),
                  _layer_cols(layer, KB, tn)],
        out_specs=pl.BlockSpec((tm, tn), lambda i, j: (i, j)),
        out_shape=jax.ShapeDtypeStruct((M, D), BF16),
        compiler_params=_cparams(("parallel", "arbitrary")),
        name="merge",
    )(projb, o_a, o_b, o_c, w_gate_up, w_gate_up, w_gate_up, b_gate, b_gate, b_gate, w_branch)


def _rope_kernel(x_ref, *refs, plan):
    o_ref = refs[-1]
    tables = refs[:-1]
    for c, (ts, half) in enumerate(plan):
        sl = slice(c * LANES, (c + 1) * LANES)
        x = x_ref[:, sl]
        cos, sin_up, sin_dn = (tables[3 * ts + t][...] for t in range(3))
        up = pltpu.roll(x, LANES - half, axis=1)
        dn = pltpu.roll(x, half, axis=1)
        o_ref[:, sl] = (x * cos + up * sin_up + dn * sin_dn).astype(o_ref.dtype)


def rope(x, plan, table_sets, *, tm=512):
    M = x.shape[0]
    tm = min(tm, M)
    width = len(plan) * LANES
    tab = pl.BlockSpec((tm, LANES), lambda i: (i, 0))
    flat = [t for ts in table_sets for t in ts]
    return pl.pallas_call(
        functools.partial(_rope_kernel, plan=tuple(plan)),
        grid=(M // tm,),
        in_specs=[pl.BlockSpec((tm, width), lambda i: (i, 0))] + [tab] * len(flat),
        out_specs=pl.BlockSpec((tm, width), lambda i: (i, 0)),
        out_shape=jax.ShapeDtypeStruct((M, width), BF16),
        compiler_params=_cparams(("parallel",)),
        name="rope",
    )(x, *flat)


def rope_tables(positions, head_dim):
    B, T = positions.shape
    rot = head_dim // ROT_FRACTION
    half = rot // 2
    inv_freq = ROPE_THETA ** (-jnp.arange(half, dtype=F32) / half)
    ang = positions.astype(F32)[:, :, None] * inv_freq
    cos, sin = jnp.cos(ang), jnp.sin(ang)
    rest = head_dim - rot
    zh = jnp.zeros((B, T, half), F32)
    c = jnp.concatenate([cos, cos, jnp.ones((B, T, rest), F32)], axis=-1)
    s1 = jnp.concatenate([-sin, zh, jnp.zeros((B, T, rest), F32)], axis=-1)
    s2 = jnp.concatenate([zh, sin, jnp.zeros((B, T, rest), F32)], axis=-1)
    reps = LANES // head_dim
    return tuple(jnp.tile(t, (1, 1, reps)).reshape(B * T, LANES) for t in (c, s1, s2)), half


def _fox_gate_kernel(a_ref, b_ref, f_ref, ft_ref, carry_ref, *, tb):
    @pl.when(pl.program_id(1) == 0)
    def _():
        carry_ref[...] = jnp.zeros_like(carry_ref)

    x = a_ref[0] + b_ref[...]
    log_f = jnp.minimum(x, 0.0) - jnp.log1p(jnp.exp(-jnp.abs(x)))
    row = lax.broadcasted_iota(I32, (tb, tb), 0)
    col = lax.broadcasted_iota(I32, (tb, tb), 1)
    tri = (col <= row).astype(F32)
    cum = jnp.dot(tri, log_f, preferred_element_type=F32,
                  precision=lax.Precision.HIGHEST) + carry_ref[...]
    cum2 = cum * LOG2E
    f_ref[0] = cum2
    ft_ref[0] = cum2.T[:16, :]
    carry_ref[...] = cum[tb - 1:tb, :]


def fox_gate(proj3, bias_row, *, tb=256):
    B, T, _ = proj3.shape
    tb = min(tb, T)
    return pl.pallas_call(
        functools.partial(_fox_gate_kernel, tb=tb),
        grid=(B, T // tb),
        in_specs=[pl.BlockSpec((1, tb, LANES), lambda b, j: (b, j, PB["a_f"])),
                  pl.BlockSpec((1, LANES), lambda b, j: (0, 0))],
        out_specs=[pl.BlockSpec((1, tb, LANES), lambda b, j: (b, j, 0)),
                   pl.BlockSpec((1, 16, tb), lambda b, j: (b, 0, j))],
        out_shape=[jax.ShapeDtypeStruct((B, T, LANES), F32),
                   jax.ShapeDtypeStruct((B, 16, T), F32)],
        scratch_shapes=[pltpu.VMEM((1, LANES), F32)],
        compiler_params=_cparams(("parallel", "arbitrary")),
        name="fox_gate",
    )(proj3, bias_row)


def _fox_kernel(q_ref, k_ref, v_ref, f_ref, ft_ref, o_ref, *, tq, tk):
    h = pl.program_id(1)
    i = pl.program_id(2)
    q = q_ref[0]
    lane = lax.broadcasted_iota(I32, (tq, LANES), 1)
    fq = jnp.sum(jnp.where(lane == h, f_ref[0], 0.0), axis=1, keepdims=True)
    nkv = (i * tq + tq + tk - 1) // tk

    def update(j, carry, diagonal):
        m, l, acc = carry
        r0 = pl.multiple_of(j * tk, tk)
        fk = ft_ref[0, 0, pl.ds(j, 1), :]
        t = _dot_tb(q, k_ref[0, pl.ds(r0, tk), :]) - fk
        if diagonal:
            qpos = i * tq + lax.broadcasted_iota(I32, (tq, tk), 0)
            kpos = j * tk + lax.broadcasted_iota(I32, (tq, tk), 1)
            t = jnp.where(kpos <= qpos, t, NEG_INF)
        m_new = jnp.maximum(m, jnp.max(t, axis=1, keepdims=True) + fq)
        p = jnp.exp2(t + (fq - m_new))
        alpha = jnp.exp2(m - m_new)
        l = alpha * l + jnp.sum(p, axis=1, keepdims=True)
        acc = alpha * acc + _dot(p.astype(BF16), v_ref[0, pl.ds(r0, tk), :])
        return m_new, l, acc

    init = (jnp.full((tq, 1), NEG_INF, F32), jnp.zeros((tq, 1), F32),
            jnp.zeros((tq, HEAD_DIM), F32))
    carry = lax.fori_loop(0, nkv - 1, lambda j, c: update(j, c, False), init)
    _, l, acc = update(nkv - 1, carry, True)
    o_ref[0] = (acc / jnp.maximum(l, 1e-30)).astype(o_ref.dtype)


def fox_attention(projb3, f, ft, *, tq=1024, tk=1024):
    B, T, _ = projb3.shape
    tq, tk = min(tq, T), min(tk, T)
    assert tk % tq == 0
    ft4 = ft.reshape(B, 16, T // tk, tk)
    return pl.pallas_call(
        functools.partial(_fox_kernel, tq=tq, tk=tk),
        grid=(B, A_HEADS, T // tq),
        in_specs=[pl.BlockSpec((1, tq, HEAD_DIM), lambda b, h, i: (b, i, PB["a_q"] + h)),
                  pl.BlockSpec((1, T, HEAD_DIM), lambda b, h, i: (b, 0, PB["a_k"] + h)),
                  pl.BlockSpec((1, T, HEAD_DIM), lambda b, h, i: (b, 0, PB["a_v"] + h)),
                  pl.BlockSpec((1, tq, LANES), lambda b, h, i: (b, i, 0)),
                  pl.BlockSpec((1, 1, T // tk, tk), lambda b, h, i: (b, h, 0, 0))],
        out_specs=pl.BlockSpec((1, tq, HEAD_DIM), lambda b, h, i: (b, i, h)),
        out_shape=jax.ShapeDtypeStruct((B, T, A_W), BF16),
        compiler_params=_cparams(("parallel", "parallel", "arbitrary")),
        name="fox_attention",
    )(projb3, projb3, projb3, f, ft4)


def _sortable_key(x):
    b = pltpu.bitcast(x, I32)
    return jnp.where(b >= 0, b, b ^ jnp.int32(0x7FFFFFFF))


def _count_ge(key_ref, ntiles, cand, rows):
    def tile(j, cnt):
        ge = jnp.where(key_ref[j] >= cand, 1, 0).astype(I32)
        return cnt + jnp.sum(ge.reshape(rows // 8, 8, LANES), axis=0)

    cnt = lax.fori_loop(0, ntiles, tile, jnp.zeros((8, LANES), I32))
    return jnp.sum(cnt, axis=0, keepdims=True)


def _kth_largest(key_ref, ntiles, k, rows):
    nonneg = _count_ge(key_ref, ntiles, jnp.zeros((1, LANES), I32), rows)
    base0 = jnp.where(nonneg >= k, 0, INT_MIN).astype(I32)

    def step(s, base):
        cand = base | lax.shift_left(jnp.int32(1), 30 - s)
        cnt = _count_ge(key_ref, ntiles, cand, rows)
        return jnp.where(cnt >= k, cand, base)

    return lax.fori_loop(0, 31, step, base0)


def _topk_need(key_ref, ntiles, thr, k, rows):
    gt = _count_ge(key_ref, ntiles, thr + 1, rows)
    gt = jnp.where(thr == jnp.int32(2 ** 31 - 1), 0, gt)
    return (k - gt).astype(F32)


def _topk_select(key, thr, need, eq_before, slt):
    eq = key == thr
    eq_f = jnp.where(eq, 1.0, 0.0)
    rank = _dot(slt, eq_f.astype(BF16)) + eq_before
    sel = (key > thr) | (eq & (rank < need))
    return sel, eq_before + jnp.sum(eq_f, axis=0, keepdims=True)


def strict_lower(n):
    r = jnp.arange(n)
    return (r[None, :] < r[:, None]).astype(BF16)


def _softmax_tile_t(s, bias, v, carry, nheads):
    m, l, acc = carry
    ms, ls, ps = [], [], []
    for h in range(nheads):
        sl = slice(h * LANES, (h + 1) * LANES)
        sh = s[:, sl] + bias
        mh = jnp.maximum(m[:, sl], jnp.max(sh, axis=0, keepdims=True))
        ph = jnp.exp2(sh - mh)
        ms.append(mh)
        ls.append(jnp.sum(ph, axis=0, keepdims=True))
        ps.append(ph.astype(BF16))
    m_new = jnp.concatenate(ms, axis=1)
    alpha = jnp.exp2(m - m_new)
    l = alpha * l + jnp.concatenate(ls, axis=1)
    acc = alpha * acc + _dot_ta(v, jnp.concatenate(ps, axis=1))
    return m_new, l, acc


def _softmax_init_t(nheads):
    width = nheads * LANES
    return (jnp.full((1, width), NEG_INF, F32), jnp.zeros((1, width), F32),
            jnp.zeros((HEAD_DIM, width), F32))


def _dsa_kernel(iq_ref, iw_ref, q_ref, ika_ref, ikb_ref, k_ref, v_ref, slt_ref, o_ref,
                iqs_ref, qs_ref, wt_ref, key_ref, *, tk, topk):
    i = pl.program_id(1)
    t0 = i * Q_BLOCK
    nkv = (t0 + Q_BLOCK + tk - 1) // tk
    npair = IDX_HEADS // 2
    for p in range(npair):
        iqs_ref[p * Q_BLOCK:(p + 1) * Q_BLOCK, :] = iq_ref[0, :, p * LANES:(p + 1) * LANES]
    for h in range(B_HEADS):
        qs_ref[h * Q_BLOCK:(h + 1) * Q_BLOCK, :] = q_ref[0, :, h * HEAD_DIM:(h + 1) * HEAD_DIM]
    wt_ref[...] = (iw_ref[0] * IDX_SCALE).T
    qpos = t0 + lax.broadcasted_iota(I32, (tk, LANES), 1)
    krow = lax.broadcasted_iota(I32, (tk, LANES), 0)
    chunk = 2

    def score_tile(j, _):
        r0 = pl.multiple_of(j * tk, tk)
        ika = ika_ref[0, pl.ds(r0, tk), :]
        ikb = ikb_ref[0, pl.ds(r0, tk), :]
        acc = jnp.zeros((tk, LANES), F32)
        for c in range(npair // chunk):
            iqc = iqs_ref[c * chunk * Q_BLOCK:(c + 1) * chunk * Q_BLOCK, :]
            ra = _dot_tb(ika, iqc)
            rb = _dot_tb(ikb, iqc)
            for u in range(chunk):
                p = c * chunk + u
                sl = slice(u * LANES, (u + 1) * LANES)
                acc = acc + jnp.maximum(ra[:, sl], 0.0) * wt_ref[2 * p:2 * p + 1, :]
                acc = acc + jnp.maximum(rb[:, sl], 0.0) * wt_ref[2 * p + 1:2 * p + 2, :]
        acc = jnp.where((j * tk + krow) <= qpos, acc, NEG_INF)
        key_ref[j] = _sortable_key(acc)
        return 0

    lax.fori_loop(0, nkv, score_tile, 0)
    thr = _kth_largest(key_ref, nkv, topk, tk)
    need = _topk_need(key_ref, nkv, thr, topk, tk)
    qs = qs_ref[...]
    slt = slt_ref[...]

    def attn_tile(j, carry, diagonal):
        eq_before = carry[3]
        r0 = pl.multiple_of(j * tk, tk)
        sel, eq_before = _topk_select(key_ref[j], thr, need, eq_before, slt)
        if diagonal:
            sel = sel & ((j * tk + krow) <= qpos)
        bias = jnp.where(sel, 0.0, NEG_INF)
        s = _dot_tb(k_ref[0, pl.ds(r0, tk), :], qs)
        return _softmax_tile_t(s, bias, v_ref[0, pl.ds(r0, tk), :], carry[:3], B_HEADS) + (eq_before,)

    init = _softmax_init_t(B_HEADS) + (jnp.zeros((1, LANES), F32),)
    carry = lax.fori_loop(0, nkv - 1, lambda j, c: attn_tile(j, c, False), init)
    _, l, acc, _ = attn_tile(nkv - 1, carry, True)
    out = acc / jnp.maximum(l, 1e-30)
    for h in range(B_HEADS):
        o_ref[0, :, h * HEAD_DIM:(h + 1) * HEAD_DIM] = (
            out[:, h * Q_BLOCK:(h + 1) * Q_BLOCK].T.astype(o_ref.dtype))


def dsa_attention(rqiq3, proj3, rp3, projb3, slt, *, topk, tk=512):
    B, T, _ = proj3.shape
    tk = min(tk, T)
    assert tk >= topk and T % tk == 0
    return pl.pallas_call(
        functools.partial(_dsa_kernel, tk=tk, topk=topk),
        grid=(B, T // Q_BLOCK),
        in_specs=[pl.BlockSpec((1, Q_BLOCK, IDX_HEADS * IDX_DIM), lambda b, i: (b, i, QB["iq"])),
                  pl.BlockSpec((1, Q_BLOCK, LANES), lambda b, i: (b, i, PB["b_iw"])),
                  pl.BlockSpec((1, Q_BLOCK, B_W), lambda b, i: (b, i, QB["q"] * LANES // B_W)),
                  pl.BlockSpec((1, T, LANES), lambda b, i: (b, 0, PB["ik_lo"])),
                  pl.BlockSpec((1, T, LANES), lambda b, i: (b, 0, PB["ik_hi"])),
                  pl.BlockSpec((1, T, HEAD_DIM), lambda b, i: (b, 0, PB["b_k"])),
                  pl.BlockSpec((1, T, HEAD_DIM), lambda b, i: (b, 0, PB["b_v"])),
                  pl.BlockSpec((tk, tk), lambda b, i: (0, 0))],
        out_specs=pl.BlockSpec((1, Q_BLOCK, B_W), lambda b, i: (b, i, 0)),
        out_shape=jax.ShapeDtypeStruct((B, T, B_W), BF16),
        scratch_shapes=[pltpu.VMEM((IDX_HEADS // 2 * Q_BLOCK, LANES), BF16),
                        pltpu.VMEM((B_HEADS * Q_BLOCK, HEAD_DIM), BF16),
                        pltpu.VMEM((LANES, Q_BLOCK), F32),
                        pltpu.VMEM((T // tk, tk, LANES), I32)],
        compiler_params=_cparams(("parallel", "arbitrary")),
        name="dsa_attention",
    )(rqiq3, proj3, rqiq3, rp3, rp3, rp3, projb3, slt)


def _compress_kernel(x_ref, pe_ref, w_ref, o_ref, *, n_slc):
    def window_sum(r, lo, hi):
        acc = jnp.zeros((n_slc, HEAD_DIM), F32)
        for l in range(lo, hi):
            off = (CMP_STRIDE * r + l) % SLC_BLOCK
            rows = x_ref[0, pl.ds(off, n_slc, stride=SLC_BLOCK), :] + pe_ref[l:l + 1, :]
            acc = acc + _dot(rows.astype(BF16), w_ref[l])
        return acc

    for r in range(CMP_PER_SLC):
        inside = min(CMP_BLOCK, SLC_BLOCK - CMP_STRIDE * r)
        acc = window_sum(r, 0, inside)
        if inside < CMP_BLOCK:
            acc = acc + pltpu.roll(window_sum(r, inside, CMP_BLOCK), n_slc - 1, axis=0)
        o_ref[0, 0, r * n_slc:(r + 1) * n_slc, :] = acc.astype(o_ref.dtype)


def nsa_compress(proj3, col_block, pe, w4, layer):
    B, T, _ = proj3.shape
    n_slc = T // SLC_BLOCK
    return pl.pallas_call(
        functools.partial(_compress_kernel, n_slc=n_slc),
        grid=(B, C_GROUPS),
        in_specs=[pl.BlockSpec((1, T, HEAD_DIM), lambda b, g: (b, 0, col_block + g)),
                  pl.BlockSpec((None, CMP_BLOCK, HEAD_DIM), lambda b, g: (layer, 0, 0)),
                  pl.BlockSpec((None, CMP_BLOCK, HEAD_DIM, HEAD_DIM), lambda b, g: (layer, 0, 0, 0))],
        out_specs=pl.BlockSpec((1, 1, CMP_PER_SLC * n_slc, HEAD_DIM), lambda b, g: (b, g, 0, 0)),
        out_shape=jax.ShapeDtypeStruct((B, C_GROUPS, CMP_PER_SLC * n_slc, HEAD_DIM), BF16),
        compiler_params=_cparams(("parallel", "parallel")),
        name="nsa_compress",
    )(proj3, pe, w4)


def _nsa_kernel(q_ref, kc_ref, vc_ref, ks_ref, vs_ref, kw_ref, vw_ref, g_ref, gb_ref,
                slt_ref, o_ref, qs_ref, gt_ref, key_ref, sb_ref, *, tk, n_slc, ksel):
    g = pl.program_id(1)
    i = pl.program_id(2)
    t0 = i * Q_BLOCK
    nh = C_GROUP_HEADS
    for h in range(nh):
        qs_ref[h * Q_BLOCK:(h + 1) * Q_BLOCK, :] = q_ref[0, :, h * HEAD_DIM:(h + 1) * HEAD_DIM]
    qs = qs_ref[...]
    gt_ref[...] = jax.nn.sigmoid(g_ref[0] + gb_ref[...]).T

    def lane_qpos(rows):
        return t0 + lax.broadcasted_iota(I32, (rows, Q_BLOCK), 1)

    nc = CMP_PER_SLC * n_slc
    crow = lax.broadcasted_iota(I32, (nc, Q_BLOCK), 0)
    cend = (crow % n_slc) * SLC_BLOCK + (crow // n_slc) * CMP_STRIDE + (CMP_BLOCK - 1)
    bias_c = jnp.where(cend <= lane_qpos(nc), 0.0, NEG_INF)
    has_cmp = lane_qpos(1) >= CMP_BLOCK - 1
    s_c = _dot_tb(kc_ref[0, 0], qs)
    ph = jnp.zeros((nc, Q_BLOCK), F32)
    p_heads = []
    for h in range(nh):
        sh = s_c[:, h * Q_BLOCK:(h + 1) * Q_BLOCK] + bias_c
        e = jnp.exp2(sh - jnp.max(sh, axis=0, keepdims=True))
        den = jnp.maximum(jnp.sum(e, axis=0, keepdims=True), 1e-30)
        pn = e * jnp.where(has_cmp, 1.0 / den, 0.0)
        ph = ph + pn
        p_heads.append(pn.astype(BF16))
    o_cmp = _dot_ta(vc_ref[0, 0], jnp.concatenate(p_heads, axis=1))

    imp = ph[0:n_slc, :]
    for r in range(1, CMP_PER_SLC):
        imp = imp + ph[r * n_slc:(r + 1) * n_slc, :]
    blk = lax.broadcasted_iota(I32, (n_slc, Q_BLOCK), 0)
    qp = lane_qpos(n_slc)
    cur = qp // SLC_BLOCK
    forced = (blk == 0) | (blk == cur) | (blk == cur - 1)
    imp = jnp.where(blk * SLC_BLOCK <= qp, imp + jnp.where(forced, FORCE_BONUS, 0.0), NEG_INF)
    key_ref[0] = _sortable_key(imp)
    thr = _kth_largest(key_ref, 1, ksel, n_slc)
    need = _topk_need(key_ref, 1, thr, ksel, n_slc)
    sel, _ = _topk_select(key_ref[0], thr, need, jnp.zeros((1, Q_BLOCK), F32), slt_ref[...])
    sb_ref[...] = jnp.where(sel, 0.0, NEG_INF)

    nkv = (t0 + Q_BLOCK + tk - 1) // tk
    blocks_per_tile = tk // SLC_BLOCK

    def sel_tile(j, carry, diagonal):
        r0 = pl.multiple_of(j * tk, tk)
        bias = jnp.concatenate(
            [jnp.broadcast_to(sb_ref[pl.ds(j * blocks_per_tile + u, 1), :], (SLC_BLOCK, Q_BLOCK))
             for u in range(blocks_per_tile)], axis=0)
        if diagonal:
            kpos = j * tk + lax.broadcasted_iota(I32, (tk, Q_BLOCK), 0)
            bias = jnp.where(kpos <= lane_qpos(tk), bias, NEG_INF)
        s = _dot_tb(ks_ref[0, pl.ds(r0, tk), :], qs)
        return _softmax_tile_t(s, bias, vs_ref[0, pl.ds(r0, tk), :], carry, nh)

    carry = lax.fori_loop(0, nkv - 1, lambda j, c: sel_tile(j, c, False), _softmax_init_t(nh))
    _, l_s, acc_s = sel_tile(nkv - 1, carry, True)
    o_sel = acc_s / jnp.maximum(l_s, 1e-30)

    span = WINDOW + Q_BLOCK
    start = pl.multiple_of(jnp.maximum(t0 - WINDOW, 0), Q_BLOCK)
    dist = lane_qpos(span) - (start + lax.broadcasted_iota(I32, (span, Q_BLOCK), 0))
    bias_w = jnp.where((dist >= 0) & (dist < WINDOW), 0.0, NEG_INF)
    s_w = _dot_tb(kw_ref[0, pl.ds(start, span), :], qs)
    _, l_w, acc_w = _softmax_tile_t(s_w, bias_w, vw_ref[0, pl.ds(start, span), :],
                                    _softmax_init_t(nh), nh)
    o_win = acc_w / jnp.maximum(l_w, 1e-30)

    for h in range(nh):
        sl = slice(h * Q_BLOCK, (h + 1) * Q_BLOCK)
        row = 3 * (g * nh + h)
        o = (gt_ref[pl.ds(row, 1), :] * o_cmp[:, sl]
             + gt_ref[pl.ds(row + 1, 1), :] * o_sel[:, sl]
             + gt_ref[pl.ds(row + 2, 1), :] * o_win[:, sl])
        o_ref[0, :, h * HEAD_DIM:(h + 1) * HEAD_DIM] = o.T.astype(o_ref.dtype)


def nsa_attention(rp3, projb3, proj3, kc, vc, gate_bias_row, slt, *, tk=512):
    B, T, _ = proj3.shape
    tk = min(tk, T)
    n_slc = T // SLC_BLOCK
    ksel = min(SLC_TOPK, n_slc)
    gw = C_GROUP_HEADS * HEAD_DIM
    assert T >= WINDOW + Q_BLOCK and T % tk == 0 and tk % SLC_BLOCK == 0

    def seq(col):
        return pl.BlockSpec((1, T, HEAD_DIM), lambda b, g, i, col=col: (b, 0, col + g))

    cmp_spec = pl.BlockSpec((1, 1, CMP_PER_SLC * n_slc, HEAD_DIM), lambda b, g, i: (b, g, 0, 0))
    return pl.pallas_call(
        functools.partial(_nsa_kernel, tk=tk, n_slc=n_slc, ksel=ksel),
        grid=(B, C_GROUPS, T // Q_BLOCK),
        in_specs=[pl.BlockSpec((1, Q_BLOCK, gw), lambda b, g, i: (b, i, g)),
                  cmp_spec, cmp_spec,
                  seq(PB["c_ks"]), seq(PB["c_vs"]), seq(PB["c_kw"]), seq(PB["c_vw"]),
                  pl.BlockSpec((1, Q_BLOCK, LANES), lambda b, g, i: (b, i, PB["c_g"])),
                  pl.BlockSpec((1, LANES), lambda b, g, i: (0, 0)),
                  pl.BlockSpec((n_slc, n_slc), lambda b, g, i: (0, 0))],
        out_specs=pl.BlockSpec((1, Q_BLOCK, gw), lambda b, g, i: (b, i, g)),
        out_shape=jax.ShapeDtypeStruct((B, T, C_W), BF16),
        scratch_shapes=[pltpu.VMEM((C_GROUP_HEADS * Q_BLOCK, HEAD_DIM), BF16),
                        pltpu.VMEM((LANES, Q_BLOCK), F32),
                        pltpu.VMEM((1, n_slc, Q_BLOCK), I32),
                        pltpu.VMEM((n_slc, Q_BLOCK), F32)],
        compiler_params=_cparams(("parallel", "parallel", "arbitrary")),
        name="nsa_attention",
    )(rp3, kc, vc, rp3, projb3, rp3, projb3, proj3, gate_bias_row, slt)


def _pad_cols(w, width):
    return jnp.pad(w, [(0, 0)] * (w.ndim - 1) + [(0, width - w.shape[-1])])


def _permute_w_in_kernel(w_ref, o_ref):
    offs = np.concatenate([[0], np.cumsum(IN_WIDTHS)])
    src = {n: (int(offs[k]), int(offs[k + 1])) for k, n in enumerate(IN_NAMES)}
    o_ref[...] = jnp.zeros_like(o_ref)
    for name, blk in PB.items():
        a, b = src["b_ik"] if name in ("ik_lo", "ik_hi") else src[name]
        d0 = blk * LANES + (IDX_DIM if name == "ik_hi" else 0)
        fold = QK_FOLD if name in ("a_q", "c_q") else 1.0
        o_ref[:, d0:d0 + (b - a)] = (w_ref[:, a:b] * fold).astype(o_ref.dtype)


def permute_w_in(w_in, *, tk=256):
    depth, D, W = w_in.shape
    tk = min(tk, D)
    return pl.pallas_call(
        _permute_w_in_kernel,
        grid=(depth, D // tk),
        in_specs=[pl.BlockSpec((None, tk, W), lambda l, i: (l, i, 0))],
        out_specs=pl.BlockSpec((None, tk, P_BLOCKS * LANES), lambda l, i: (l, i, 0)),
        out_shape=jax.ShapeDtypeStruct((depth, D, P_BLOCKS * LANES), BF16),
        compiler_params=_cparams(("parallel", "parallel")),
        name="permute_w_in",
    )(w_in)


def _lane_row(v):
    return _pad_cols(v.reshape(1, -1).astype(F32), LANES)


def kernel(x, positions, g_mix_pre, g_mix_post, g_ffn_pre, g_ffn_post, w_in, fox_forget_bias,
           dsa_q_norm, w_dsa_uq, w_dsa_iq, nsa_cmp_pos_k, nsa_cmp_pos_v, w_nsa_cmp_k,
           w_nsa_cmp_v, nsa_gate_bias, w_gate_up, b_gate, w_branch, w_out, w_ffn_gate,
           w_ffn_up, w_ffn_down):
    B, T, D = x.shape
    depth = w_in.shape[0]
    M = B * T
    topk_b = min(IDX_TOPK_MAX, T // 4)
    n_slc = T // SLC_BLOCK
    tk = min(512, T)

    tab128, half128 = rope_tables(positions, HEAD_DIM)
    tab64, half64 = rope_tables(positions, IDX_DIM)
    tables = (tab128, tab64)
    proj_plan = [(0, half128)] * PB["ik_lo"] + [(1, half64)] * (ROPE_BLOCKS - PB["ik_lo"])
    q_plan = [(1, half64)] * QB["q"] + [(0, half128)] * B_HEADS
    slt_tk = strict_lower(tk)
    slt_blk = strict_lower(n_slc)
    d_ff = w_ffn_gate.shape[-1]
    ff_tn = next(t for t in (512, 256, 128) if d_ff % t == 0)
    ff_tk = d_ff // 2 if (d_ff // 2) % LANES == 0 else d_ff

    w_in_b = permute_w_in(w_in)
    w_q_b = jnp.concatenate([w_dsa_iq, w_dsa_uq * QK_FOLD], axis=-1).astype(BF16)
    w_ck_b = w_nsa_cmp_k.reshape(depth, CMP_BLOCK, HEAD_DIM, HEAD_DIM).astype(BF16)
    w_cv_b = w_nsa_cmp_v.reshape(depth, CMP_BLOCK, HEAD_DIM, HEAD_DIM).astype(BF16)
    w_gate_b = w_gate_up.astype(BF16)
    b_gate3 = b_gate.reshape(depth, 1, -1)
    w_branch_b = w_branch.astype(BF16)
    w_out_b = w_out.astype(BF16)
    w_fg_b = w_ffn_gate.astype(BF16)
    w_fu_b = w_ffn_up.astype(BF16)
    w_fd_b = w_ffn_down.astype(BF16)

    xf = x.reshape(M, D)
    h = rmsnorm(xf, g_mix_pre[0], width=D)
    for l in range(depth):
        proj, projb = matmul(h, w_in_b, l, (F32, BF16))
        proj3 = proj.reshape(B, T, -1)
        projb3 = projb.reshape(B, T, -1)
        rp3 = rope(proj, proj_plan, tables).reshape(B, T, -1)

        f, ft = fox_gate(proj3, _lane_row(fox_forget_bias[l]))
        o_a = fox_attention(projb3, f, ft)

        cqn = rmsnorm(proj, dsa_q_norm[l], width=B_Q_LORA, col_block=PB["b_cq"] * LANES // B_Q_LORA)
        (qiq,) = matmul(cqn, w_q_b, l, (F32,), tn=w_q_b.shape[2] // 2)
        rqiq3 = rope(qiq, q_plan, tables).reshape(B, T, -1)
        o_b = dsa_attention(rqiq3, proj3, rp3, projb3, slt_tk, topk=topk_b, tk=tk)

        kc = nsa_compress(proj3, PB["c_kc"], nsa_cmp_pos_k, w_ck_b, l)
        vc = nsa_compress(proj3, PB["c_vc"], nsa_cmp_pos_v, w_cv_b, l)
        o_c = nsa_attention(rp3, projb3, proj3, kc, vc, _lane_row(nsa_gate_bias[l]),
                            slt_blk, tk=min(1024, T))

        y = merge(projb, o_a.reshape(M, -1), o_b.reshape(M, -1), o_c.reshape(M, -1),
                  w_gate_b, b_gate3, w_branch_b, l)
        (mix,) = matmul(y, w_out_b, l, (F32,))
        xf, h = resnorm(xf, mix, g_mix_post[l], g_ffn_pre[l])

        u = ffn_up(h, w_fg_b, w_fu_b, l, tn=ff_tn)
        ffn = matmul_ksplit(u, w_fd_b, l, F32, tm=1024, tn=512, tk=ff_tk)
        if l + 1 < depth:
            xf, h = resnorm(xf, ffn, g_ffn_post[l], g_mix_pre[l + 1])
        else:
            xf = resnorm(xf, ffn, g_ffn_post[l])
    return xf.reshape(B, T, D)
```

```python
import functools

import jax
import jax.numpy as jnp
import numpy as np
from jax import lax
from jax.experimental import pallas as pl
from jax.experimental.pallas import tpu as pltpu

F32 = jnp.float32
BF16 = jnp.bfloat16
I32 = jnp.int32

HEAD_DIM = 128
LANES = 128
ROPE_THETA = 500000.0
ROT_FRACTION = 4
NORM_EPS = 1e-6
NEG_INF = -1e30
INT_MIN = -(2 ** 31)

A_HEADS = 12
A_W = A_HEADS * HEAD_DIM
B_HEADS = 8
B_W = B_HEADS * HEAD_DIM
B_Q_LORA = 1024
IDX_HEADS = 32
IDX_DIM = 64
IDX_TOPK_MAX = 256
IDX_SCALE = (IDX_HEADS * IDX_DIM) ** -0.5
C_HEADS = 12
C_W = C_HEADS * HEAD_DIM
C_GROUPS = 2
C_GROUP_HEADS = C_HEADS // C_GROUPS
C_KV_W = C_GROUPS * HEAD_DIM
CMP_BLOCK = 32
CMP_STRIDE = 16
SLC_BLOCK = 64
SLC_TOPK = 16
CMP_PER_SLC = SLC_BLOCK // CMP_STRIDE
WINDOW = 512
FORCE_BONUS = 1e4
GATE_RANK = 512
Q_BLOCK = 128
LOG2E = 1.4426950408889634
QK_FOLD = HEAD_DIM ** -0.5 * LOG2E

IN_WIDTHS = (A_W, A_W, A_W, A_HEADS,
             B_Q_LORA, HEAD_DIM, HEAD_DIM, IDX_DIM, IDX_HEADS,
             C_W, C_KV_W, C_KV_W, C_KV_W, C_KV_W, C_KV_W, C_KV_W, 3 * C_HEADS,
             GATE_RANK)
IN_NAMES = ("a_q", "a_k", "a_v", "a_f", "b_cq", "b_k", "b_v", "b_ik", "b_iw",
            "c_q", "c_kc", "c_vc", "c_ks", "c_vs", "c_kw", "c_vw", "c_g", "z")

PB = dict(c_q=0, b_k=12, c_ks=13, c_kw=15, ik_lo=17, ik_hi=18, b_v=19, z=20, b_cq=24,
          a_q=32, a_k=44, a_v=56, c_kc=68, c_vc=70, c_vs=72, c_vw=74,
          a_f=76, b_iw=77, c_g=78)
P_BLOCKS = 80
ROPE_BLOCKS = 19
QB = dict(iq=0, q=IDX_HEADS * IDX_DIM // LANES)

VMEM_LIMIT = 56 * 1024 * 1024


def _cparams(sem, vmem=VMEM_LIMIT):
    return pltpu.CompilerParams(dimension_semantics=sem, vmem_limit_bytes=vmem)


def _dot(a, b):
    return jnp.dot(a, b, preferred_element_type=F32)


def _dot_tb(a, b):
    return lax.dot_general(a, b, (((1,), (1,)), ((), ())), preferred_element_type=F32)


def _dot_ta(a, b):
    return lax.dot_general(a, b, (((0,), (0,)), ((), ())), preferred_element_type=F32)


def _rmsnorm_kernel(x_ref, g_ref, o_ref):
    x = x_ref[...].astype(F32)
    y = x * lax.rsqrt(jnp.mean(x * x, axis=-1, keepdims=True) + NORM_EPS)
    o_ref[...] = (y * g_ref[...]).astype(o_ref.dtype)


def rmsnorm(x, g, *, width, col_block=0, tm=512):
    M = x.shape[0]
    tm = min(tm, M)
    return pl.pallas_call(
        _rmsnorm_kernel,
        grid=(M // tm,),
        in_specs=[pl.BlockSpec((tm, width), lambda i: (i, col_block)),
                  pl.BlockSpec((1, width), lambda i: (0, 0))],
        out_specs=pl.BlockSpec((tm, width), lambda i: (i, 0)),
        out_shape=jax.ShapeDtypeStruct((M, width), BF16),
        compiler_params=_cparams(("parallel",)),
        name="rmsnorm",
    )(x, g.reshape(1, width).astype(F32))


def _resnorm_kernel(x_ref, m_ref, g1_ref, g2_ref, xo_ref, ho_ref):
    m = m_ref[...]
    y = m * lax.rsqrt(jnp.mean(m * m, axis=-1, keepdims=True) + NORM_EPS) * g1_ref[...]
    xn = x_ref[...] + y
    xo_ref[...] = xn
    h = xn * lax.rsqrt(jnp.mean(xn * xn, axis=-1, keepdims=True) + NORM_EPS) * g2_ref[...]
    ho_ref[...] = h.astype(ho_ref.dtype)


def _resnorm_last_kernel(x_ref, m_ref, g1_ref, xo_ref):
    m = m_ref[...]
    y = m * lax.rsqrt(jnp.mean(m * m, axis=-1, keepdims=True) + NORM_EPS) * g1_ref[...]
    xo_ref[...] = x_ref[...] + y


def resnorm(x, m, g1, g2=None, *, tm=256):
    M, D = x.shape
    tm = min(tm, M)
    row = pl.BlockSpec((tm, D), lambda i: (i, 0))
    gain = pl.BlockSpec((1, D), lambda i: (0, 0))
    if g2 is None:
        return pl.pallas_call(
            _resnorm_last_kernel, grid=(M // tm,),
            in_specs=[row, row, gain], out_specs=row,
            out_shape=jax.ShapeDtypeStruct((M, D), F32),
            compiler_params=_cparams(("parallel",)), name="resnorm_last",
        )(x, m, g1.reshape(1, D))
    return pl.pallas_call(
        _resnorm_kernel, grid=(M // tm,),
        in_specs=[row, row, gain, gain], out_specs=[row, row],
        out_shape=[jax.ShapeDtypeStruct((M, D), F32), jax.ShapeDtypeStruct((M, D), BF16)],
        compiler_params=_cparams(("parallel",)), name="resnorm",
    )(x, m, g1.reshape(1, D), g2.reshape(1, D))


def _mm_kernel(x_ref, w_ref, *o_refs):
    acc = _dot(x_ref[...], w_ref[...].astype(BF16))
    for o_ref in o_refs:
        o_ref[...] = acc.astype(o_ref.dtype)


def _layer_cols(layer, K, tn):
    return pl.BlockSpec((None, K, tn), lambda i, j: (layer, 0, j))


def matmul(x, w, layer, out_dtypes, *, tm=1024, tn=512):
    M, K = x.shape
    N = w.shape[2]
    tm, tn = min(tm, M), min(tn, N)
    outs = pl.pallas_call(
        _mm_kernel,
        grid=(M // tm, N // tn),
        in_specs=[pl.BlockSpec((tm, K), lambda i, j: (i, 0)), _layer_cols(layer, K, tn)],
        out_specs=[pl.BlockSpec((tm, tn), lambda i, j: (i, j)) for _ in out_dtypes],
        out_shape=[jax.ShapeDtypeStruct((M, N), dt) for dt in out_dtypes],
        compiler_params=_cparams(("parallel", "arbitrary")),
        name="matmul",
    )(x, w)
    return outs


def _mm_ksplit_kernel(x_ref, w_ref, o_ref, acc_ref):
    k = pl.program_id(2)

    @pl.when(k == 0)
    def _():
        acc_ref[...] = jnp.zeros_like(acc_ref)

    acc_ref[...] += _dot(x_ref[...], w_ref[...].astype(BF16))

    @pl.when(k == pl.num_programs(2) - 1)
    def _():
        o_ref[...] = acc_ref[...].astype(o_ref.dtype)


def matmul_ksplit(x, w, layer, out_dtype, *, tm, tn, tk):
    M, K = x.shape
    N = w.shape[2]
    tm, tn, tk = min(tm, M), min(tn, N), min(tk, K)
    return pl.pallas_call(
        _mm_ksplit_kernel,
        grid=(M // tm, N // tn, K // tk),
        in_specs=[pl.BlockSpec((tm, tk), lambda i, j, k: (i, k)),
                  pl.BlockSpec((None, tk, tn), lambda i, j, k: (layer, k, j))],
        out_specs=pl.BlockSpec((tm, tn), lambda i, j, k: (i, j)),
        out_shape=jax.ShapeDtypeStruct((M, N), out_dtype),
        scratch_shapes=[pltpu.VMEM((tm, tn), F32)],
        compiler_params=_cparams(("parallel", "arbitrary", "arbitrary")),
        name="matmul_ksplit",
    )(x, w)


def _ffn_up_kernel(h_ref, wg_ref, wu_ref, o_ref):
    h = h_ref[...]
    g = _dot(h, wg_ref[...].astype(BF16))
    u = _dot(h, wu_ref[...].astype(BF16))
    o_ref[...] = (g * jax.nn.sigmoid(g) * u).astype(o_ref.dtype)


def ffn_up(h, wg, wu, layer, *, tm=1024, tn=256):
    M, K = h.shape
    N = wg.shape[2]
    tm, tn = min(tm, M), min(tn, N)
    return pl.pallas_call(
        _ffn_up_kernel,
        grid=(M // tm, N // tn),
        in_specs=[pl.BlockSpec((tm, K), lambda i, j: (i, 0)),
                  _layer_cols(layer, K, tn), _layer_cols(layer, K, tn)],
        out_specs=pl.BlockSpec((tm, tn), lambda i, j: (i, j)),
        out_shape=jax.ShapeDtypeStruct((M, N), BF16),
        compiler_params=_cparams(("parallel", "arbitrary")),
        name="ffn_up",
    )(h, wg, wu)


def _merge_kernel(z_ref, oa_ref, ob_ref, oc_ref, wga_ref, wgb_ref, wgc_ref,
                  bga_ref, bgb_ref, bgc_ref, wbr_ref, y_ref):
    z = z_ref[...]

    def branch(o_ref, row0, wg_ref, bg_ref):
        gate = jax.nn.sigmoid(_dot(z, wg_ref[...].astype(BF16)) + bg_ref[...])
        rows = o_ref.shape[1]
        return gate * _dot(o_ref[...], wbr_ref[row0:row0 + rows, :].astype(BF16))

    y = (branch(oa_ref, 0, wga_ref, bga_ref)
         + branch(ob_ref, A_W, wgb_ref, bgb_ref)
         + branch(oc_ref, A_W + B_W, wgc_ref, bgc_ref))
    y_ref[...] = y.astype(y_ref.dtype)


def merge(projb, o_a, o_b, o_c, w_gate_up, b_gate, w_branch, layer, *, tm=1024, tn=512):
    M = projb.shape[0]
    KB, D = w_branch.shape[1:]
    tm, tn = min(tm, M), min(tn, D)
    nj = D // tn
    z_blk = PB["z"] * LANES // GATE_RANK

    def full_k(o):
        return pl.BlockSpec((tm, o.shape[1]), lambda i, j: (i, 0))

    def gate_w(b):
        return pl.BlockSpec((None, GATE_RANK, tn), lambda i, j, b=b: (layer, 0, b * nj + j))

    def gate_b(b):
        return pl.BlockSpec((None, 1, tn), lambda i, j, b=b: (layer, 0, b * nj + j))

    return pl.pallas_call(
        _merge_kernel,
        grid=(M // tm, nj),
        in_specs=[pl.BlockSpec((tm, GATE_RANK), lambda i, j: (i, z_blk)),
                  full_k(o_a), full_k(o_b), full_k(o_c),
                  gate_w(0), gate_w(1), gate_w(2), gate_b(0), gate_b(1), gate_b(2),
                  _layer_cols(layer, KB, tn)],
        out_specs=pl.BlockSpec((tm, tn), lambda i, j: (i, j)),
        out_shape=jax.ShapeDtypeStruct((M, D), BF16),
        compiler_params=_cparams(("parallel", "arbitrary")),
        name="merge",
    )(projb, o_a, o_b, o_c, w_gate_up, w_gate_up, w_gate_up, b_gate, b_gate, b_gate, w_branch)


def _rope_kernel(x_ref, *refs, plan):
    o_ref = refs[-1]
    tables = refs[:-1]
    for c, (ts, half) in enumerate(plan):
        sl = slice(c * LANES, (c + 1) * LANES)
        x = x_ref[:, sl]
        cos, sin_up, sin_dn = (tables[3 * ts + t][...] for t in range(3))
        up = pltpu.roll(x, LANES - half, axis=1)
        dn = pltpu.roll(x, half, axis=1)
        o_ref[:, sl] = (x * cos + up * sin_up + dn * sin_dn).astype(o_ref.dtype)


def rope(x, plan, table_sets, *, tm=512):
    M = x.shape[0]
    tm = min(tm, M)
    width = len(plan) * LANES
    tab = pl.BlockSpec((tm, LANES), lambda i: (i, 0))
    flat = [t for ts in table_sets for t in ts]
    return pl.pallas_call(
        functools.partial(_rope_kernel, plan=tuple(plan)),
        grid=(M // tm,),
        in_specs=[pl.BlockSpec((tm, width), lambda i: (i, 0))] + [tab] * len(flat),
        out_specs=pl.BlockSpec((tm, width), lambda i: (i, 0)),
        out_shape=jax.ShapeDtypeStruct((M, width), BF16),
        compiler_params=_cparams(("parallel",)),
        name="rope",
    )(x, *flat)


def rope_tables(positions, head_dim):
    B, T = positions.shape
    rot = head_dim // ROT_FRACTION
    half = rot // 2
    inv_freq = ROPE_THETA ** (-jnp.arange(half, dtype=F32) / half)
    ang = positions.astype(F32)[:, :, None] * inv_freq
    cos, sin = jnp.cos(ang), jnp.sin(ang)
    rest = head_dim - rot
    zh = jnp.zeros((B, T, half), F32)
    c = jnp.concatenate([cos, cos, jnp.ones((B, T, rest), F32)], axis=-1)
    s1 = jnp.concatenate([-sin, zh, jnp.zeros((B, T, rest), F32)], axis=-1)
    s2 = jnp.concatenate([zh, sin, jnp.zeros((B, T, rest), F32)], axis=-1)
    reps = LANES // head_dim
    return tuple(jnp.tile(t, (1, 1, reps)).reshape(B * T, LANES) for t in (c, s1, s2)), half


def _fox_gate_kernel(a_ref, b_ref, f_ref, ft_ref, carry_ref, *, tb):
    @pl.when(pl.program_id(1) == 0)
    def _():
        carry_ref[...] = jnp.zeros_like(carry_ref)

    x = a_ref[0] + b_ref[...]
    log_f = jnp.minimum(x, 0.0) - jnp.log1p(jnp.exp(-jnp.abs(x)))
    row = lax.broadcasted_iota(I32, (tb, tb), 0)
    col = lax.broadcasted_iota(I32, (tb, tb), 1)
    tri = (col <= row).astype(F32)
    cum = jnp.dot(tri, log_f, preferred_element_type=F32,
                  precision=lax.Precision.HIGHEST) + carry_ref[...]
    cum2 = cum * LOG2E
    f_ref[0] = cum2
    ft_ref[0] = cum2.T[:16, :]
    carry_ref[...] = cum[tb - 1:tb, :]


def fox_gate(proj3, bias_row, *, tb=256):
    B, T, _ = proj3.shape
    tb = min(tb, T)
    return pl.pallas_call(
        functools.partial(_fox_gate_kernel, tb=tb),
        grid=(B, T // tb),
        in_specs=[pl.BlockSpec((1, tb, LANES), lambda b, j: (b, j, PB["a_f"])),
                  pl.BlockSpec((1, LANES), lambda b, j: (0, 0))],
        out_specs=[pl.BlockSpec((1, tb, LANES), lambda b, j: (b, j, 0)),
                   pl.BlockSpec((1, 16, tb), lambda b, j: (b, 0, j))],
        out_shape=[jax.ShapeDtypeStruct((B, T, LANES), F32),
                   jax.ShapeDtypeStruct((B, 16, T), F32)],
        scratch_shapes=[pltpu.VMEM((1, LANES), F32)],
        compiler_params=_cparams(("parallel", "arbitrary")),
        name="fox_gate",
    )(proj3, bias_row)


def _fox_kernel(q_ref, k_ref, v_ref, f_ref, ft_ref, o_ref, *, tq, tk):
    h = pl.program_id(1)
    i = pl.program_id(2)
    q = q_ref[0]
    lane = lax.broadcasted_iota(I32, (tq, LANES), 1)
    fq = jnp.sum(jnp.where(lane == h, f_ref[0], 0.0), axis=1, keepdims=True)
    nkv = (i * tq + tq + tk - 1) // tk

    def update(j, carry, diagonal):
        m, l, acc = carry
        r0 = pl.multiple_of(j * tk, tk)
        fk = ft_ref[0, 0, pl.ds(j, 1), :]
        t = _dot_tb(q, k_ref[0, pl.ds(r0, tk), :]) - fk
        if diagonal:
            qpos = i * tq + lax.broadcasted_iota(I32, (tq, tk), 0)
            kpos = j * tk + lax.broadcasted_iota(I32, (tq, tk), 1)
            t = jnp.where(kpos <= qpos, t, NEG_INF)
        m_new = jnp.maximum(m, jnp.max(t, axis=1, keepdims=True) + fq)
        p = jnp.exp2(t + (fq - m_new))
        alpha = jnp.exp2(m - m_new)
        l = alpha * l + jnp.sum(p, axis=1, keepdims=True)
        acc = alpha * acc + _dot(p.astype(BF16), v_ref[0, pl.ds(r0, tk), :])
        return m_new, l, acc

    init = (jnp.full((tq, 1), NEG_INF, F32), jnp.zeros((tq, 1), F32),
            jnp.zeros((tq, HEAD_DIM), F32))
    carry = lax.fori_loop(0, nkv - 1, lambda j, c: update(j, c, False), init)
    _, l, acc = update(nkv - 1, carry, True)
    o_ref[0] = (acc / jnp.maximum(l, 1e-30)).astype(o_ref.dtype)


def fox_attention(projb3, f, ft, *, tq=1024, tk=1024):
    B, T, _ = projb3.shape
    tq, tk = min(tq, T), min(tk, T)
    assert tk % tq == 0
    ft4 = ft.reshape(B, 16, T // tk, tk)
    return pl.pallas_call(
        functools.partial(_fox_kernel, tq=tq, tk=tk),
        grid=(B, A_HEADS, T // tq),
        in_specs=[pl.BlockSpec((1, tq, HEAD_DIM), lambda b, h, i: (b, i, PB["a_q"] + h)),
                  pl.BlockSpec((1, T, HEAD_DIM), lambda b, h, i: (b, 0, PB["a_k"] + h)),
                  pl.BlockSpec((1, T, HEAD_DIM), lambda b, h, i: (b, 0, PB["a_v"] + h)),
                  pl.BlockSpec((1, tq, LANES), lambda b, h, i: (b, i, 0)),
                  pl.BlockSpec((1, 1, T // tk, tk), lambda b, h, i: (b, h, 0, 0))],
        out_specs=pl.BlockSpec((1, tq, HEAD_DIM), lambda b, h, i: (b, i, h)),
        out_shape=jax.ShapeDtypeStruct((B, T, A_W), BF16),
        compiler_params=_cparams(("parallel", "parallel", "arbitrary")),
        name="fox_attention",
    )(projb3, projb3, projb3, f, ft4)


def _sortable_key(x):
    b = pltpu.bitcast(x, I32)
    return jnp.where(b >= 0, b, b ^ jnp.int32(0x7FFFFFFF))


def _count_ge(key_ref, ntiles, cand, rows):
    def tile(j, cnt):
        ge = jnp.where(key_ref[j] >= cand, 1, 0).astype(I32)
        return cnt + jnp.sum(ge.reshape(rows // 8, 8, LANES), axis=0)

    cnt = lax.fori_loop(0, ntiles, tile, jnp.zeros((8, LANES), I32))
    return jnp.sum(cnt, axis=0, keepdims=True)


def _kth_largest(key_ref, ntiles, k, rows):
    nonneg = _count_ge(key_ref, ntiles, jnp.zeros((1, LANES), I32), rows)
    base0 = jnp.where(nonneg >= k, 0, INT_MIN).astype(I32)

    def step(s, base):
        cand = base | lax.shift_left(jnp.int32(1), 30 - s)
        cnt = _count_ge(key_ref, ntiles, cand, rows)
        return jnp.where(cnt >= k, cand, base)

    return lax.fori_loop(0, 31, step, base0)


def _topk_need(key_ref, ntiles, thr, k, rows):
    gt = _count_ge(key_ref, ntiles, thr + 1, rows)
    gt = jnp.where(thr == jnp.int32(2 ** 31 - 1), 0, gt)
    return (k - gt).astype(F32)


def _topk_select(key, thr, need, eq_before, slt):
    eq = key == thr
    eq_f = jnp.where(eq, 1.0, 0.0)
    rank = _dot(slt, eq_f.astype(BF16)) + eq_before
    sel = (key > thr) | (eq & (rank < need))
    return sel, eq_before + jnp.sum(eq_f, axis=0, keepdims=True)


def strict_lower(n):
    r = jnp.arange(n)
    return (r[None, :] < r[:, None]).astype(BF16)


def _softmax_tile_t(s, bias, v, carry, nheads):
    m, l, acc = carry
    ms, ls, ps = [], [], []
    for h in range(nheads):
        sl = slice(h * LANES, (h + 1) * LANES)
        sh = s[:, sl] + bias
        mh = jnp.maximum(m[:, sl], jnp.max(sh, axis=0, keepdims=True))
        ph = jnp.exp2(sh - mh)
        ms.append(mh)
        ls.append(jnp.sum(ph, axis=0, keepdims=True))
        ps.append(ph.astype(BF16))
    m_new = jnp.concatenate(ms, axis=1)
    alpha = jnp.exp2(m - m_new)
    l = alpha * l + jnp.concatenate(ls, axis=1)
    acc = alpha * acc + _dot_ta(v, jnp.concatenate(ps, axis=1))
    return m_new, l, acc


def _softmax_init_t(nheads):
    width = nheads * LANES
    return (jnp.full((1, width), NEG_INF, F32), jnp.zeros((1, width), F32),
            jnp.zeros((HEAD_DIM, width), F32))


def _dsa_kernel(iq_ref, iw_ref, q_ref, ika_ref, ikb_ref, k_ref, v_ref, slt_ref, o_ref,
                iqs_ref, qs_ref, wt_ref, key_ref, *, tk, topk):
    i = pl.program_id(1)
    t0 = i * Q_BLOCK
    nkv = (t0 + Q_BLOCK + tk - 1) // tk
    npair = IDX_HEADS // 2
    for p in range(npair):
        iqs_ref[p * Q_BLOCK:(p + 1) * Q_BLOCK, :] = iq_ref[0, :, p * LANES:(p + 1) * LANES]
    for h in range(B_HEADS):
        qs_ref[h * Q_BLOCK:(h + 1) * Q_BLOCK, :] = q_ref[0, :, h * HEAD_DIM:(h + 1) * HEAD_DIM]
    wt_ref[...] = (iw_ref[0] * IDX_SCALE).T
    qpos = t0 + lax.broadcasted_iota(I32, (tk, LANES), 1)
    krow = lax.broadcasted_iota(I32, (tk, LANES), 0)
    chunk = 2

    def score_tile(j, _):
        r0 = pl.multiple_of(j * tk, tk)
        ika = ika_ref[0, pl.ds(r0, tk), :]
        ikb = ikb_ref[0, pl.ds(r0, tk), :]
        acc = jnp.zeros((tk, LANES), F32)
        for c in range(npair // chunk):
            iqc = iqs_ref[c * chunk * Q_BLOCK:(c + 1) * chunk * Q_BLOCK, :]
            ra = _dot_tb(ika, iqc)
            rb = _dot_tb(ikb, iqc)
            for u in range(chunk):
                p = c * chunk + u
                sl = slice(u * LANES, (u + 1) * LANES)
                acc = acc + jnp.maximum(ra[:, sl], 0.0) * wt_ref[2 * p:2 * p + 1, :]
                acc = acc + jnp.maximum(rb[:, sl], 0.0) * wt_ref[2 * p + 1:2 * p + 2, :]
        acc = jnp.where((j * tk + krow) <= qpos, acc, NEG_INF)
        key_ref[j] = _sortable_key(acc)
        return 0

    lax.fori_loop(0, nkv, score_tile, 0)
    thr = _kth_largest(key_ref, nkv, topk, tk)
    need = _topk_need(key_ref, nkv, thr, topk, tk)
    qs = qs_ref[...]
    slt = slt_ref[...]

    def attn_tile(j, carry, diagonal):
        eq_before = carry[3]
        r0 = pl.multiple_of(j * tk, tk)
        sel, eq_before = _topk_select(key_ref[j], thr, need, eq_before, slt)
        if diagonal:
            sel = sel & ((j * tk + krow) <= qpos)
        bias = jnp.where(sel, 0.0, NEG_INF)
        s = _dot_tb(k_ref[0, pl.ds(r0, tk), :], qs)
        return _softmax_tile_t(s, bias, v_ref[0, pl.ds(r0, tk), :], carry[:3], B_HEADS) + (eq_before,)

    init = _softmax_init_t(B_HEADS) + (jnp.zeros((1, LANES), F32),)
    carry = lax.fori_loop(0, nkv - 1, lambda j, c: attn_tile(j, c, False), init)
    _, l, acc, _ = attn_tile(nkv - 1, carry, True)
    out = acc / jnp.maximum(l, 1e-30)
    for h in range(B_HEADS):
        o_ref[0, :, h * HEAD_DIM:(h + 1) * HEAD_DIM] = (
            out[:, h * Q_BLOCK:(h + 1) * Q_BLOCK].T.astype(o_ref.dtype))


def dsa_attention(rqiq3, proj3, rp3, projb3, slt, *, topk, tk=512):
    B, T, _ = proj3.shape
    tk = min(tk, T)
    assert tk >= topk and T % tk == 0
    return pl.pallas_call(
        functools.partial(_dsa_kernel, tk=tk, topk=topk),
        grid=(B, T // Q_BLOCK),
        in_specs=[pl.BlockSpec((1, Q_BLOCK, IDX_HEADS * IDX_DIM), lambda b, i: (b, i, QB["iq"])),
                  pl.BlockSpec((1, Q_BLOCK, LANES), lambda b, i: (b, i, PB["b_iw"])),
                  pl.BlockSpec((1, Q_BLOCK, B_W), lambda b, i: (b, i, QB["q"] * LANES // B_W)),
                  pl.BlockSpec((1, T, LANES), lambda b, i: (b, 0, PB["ik_lo"])),
                  pl.BlockSpec((1, T, LANES), lambda b, i: (b, 0, PB["ik_hi"])),
                  pl.BlockSpec((1, T, HEAD_DIM), lambda b, i: (b, 0, PB["b_k"])),
                  pl.BlockSpec((1, T, HEAD_DIM), lambda b, i: (b, 0, PB["b_v"])),
                  pl.BlockSpec((tk, tk), lambda b, i: (0, 0))],
        out_specs=pl.BlockSpec((1, Q_BLOCK, B_W), lambda b, i: (b, i, 0)),
        out_shape=jax.ShapeDtypeStruct((B, T, B_W), BF16),
        scratch_shapes=[pltpu.VMEM((IDX_HEADS // 2 * Q_BLOCK, LANES), BF16),
                        pltpu.VMEM((B_HEADS * Q_BLOCK, HEAD_DIM), BF16),
                        pltpu.VMEM((LANES, Q_BLOCK), F32),
                        pltpu.VMEM((T // tk, tk, LANES), I32)],
        compiler_params=_cparams(("parallel", "arbitrary")),
        name="dsa_attention",
    )(rqiq3, proj3, rqiq3, rp3, rp3, rp3, projb3, slt)


def _compress_kernel(x_ref, pe_ref, w_ref, o_ref, *, n_slc):
    def window_sum(r, lo, hi):
        acc = jnp.zeros((n_slc, HEAD_DIM), F32)
        for l in range(lo, hi):
            off = (CMP_STRIDE * r + l) % SLC_BLOCK
            rows = x_ref[0, pl.ds(off, n_slc, stride=SLC_BLOCK), :] + pe_ref[l:l + 1, :]
            acc = acc + _dot(rows.astype(BF16), w_ref[l].astype(BF16))
        return acc

    for r in range(CMP_PER_SLC):
        inside = min(CMP_BLOCK, SLC_BLOCK - CMP_STRIDE * r)
        acc = window_sum(r, 0, inside)
        if inside < CMP_BLOCK:
            acc = acc + pltpu.roll(window_sum(r, inside, CMP_BLOCK), n_slc - 1, axis=0)
        o_ref[0, 0, r * n_slc:(r + 1) * n_slc, :] = acc.astype(o_ref.dtype)


def nsa_compress(proj3, col_block, pe, w4, layer):
    B, T, _ = proj3.shape
    n_slc = T // SLC_BLOCK
    return pl.pallas_call(
        functools.partial(_compress_kernel, n_slc=n_slc),
        grid=(B, C_GROUPS),
        in_specs=[pl.BlockSpec((1, T, HEAD_DIM), lambda b, g: (b, 0, col_block + g)),
                  pl.BlockSpec((None, CMP_BLOCK, HEAD_DIM), lambda b, g: (layer, 0, 0)),
                  pl.BlockSpec((None, CMP_BLOCK, HEAD_DIM, HEAD_DIM), lambda b, g: (layer, 0, 0, 0))],
        out_specs=pl.BlockSpec((1, 1, CMP_PER_SLC * n_slc, HEAD_DIM), lambda b, g: (b, g, 0, 0)),
        out_shape=jax.ShapeDtypeStruct((B, C_GROUPS, CMP_PER_SLC * n_slc, HEAD_DIM), BF16),
        compiler_params=_cparams(("parallel", "parallel")),
        name="nsa_compress",
    )(proj3, pe, w4)


def _nsa_kernel(q_ref, kc_ref, vc_ref, ks_ref, vs_ref, kw_ref, vw_ref, g_ref, gb_ref,
                slt_ref, o_ref, qs_ref, gt_ref, key_ref, sb_ref, *, tk, n_slc, ksel):
    g = pl.program_id(1)
    i = pl.program_id(2)
    t0 = i * Q_BLOCK
    nh = C_GROUP_HEADS
    for h in range(nh):
        qs_ref[h * Q_BLOCK:(h + 1) * Q_BLOCK, :] = q_ref[0, :, h * HEAD_DIM:(h + 1) * HEAD_DIM]
    qs = qs_ref[...]
    gt_ref[...] = jax.nn.sigmoid(g_ref[0] + gb_ref[...]).T

    def lane_qpos(rows):
        return t0 + lax.broadcasted_iota(I32, (rows, Q_BLOCK), 1)

    nc = CMP_PER_SLC * n_slc
    crow = lax.broadcasted_iota(I32, (nc, Q_BLOCK), 0)
    cend = (crow % n_slc) * SLC_BLOCK + (crow // n_slc) * CMP_STRIDE + (CMP_BLOCK - 1)
    bias_c = jnp.where(cend <= lane_qpos(nc), 0.0, NEG_INF)
    has_cmp = lane_qpos(1) >= CMP_BLOCK - 1
    s_c = _dot_tb(kc_ref[0, 0], qs)
    ph = jnp.zeros((nc, Q_BLOCK), F32)
    p_heads = []
    for h in range(nh):
        sh = s_c[:, h * Q_BLOCK:(h + 1) * Q_BLOCK] + bias_c
        e = jnp.exp2(sh - jnp.max(sh, axis=0, keepdims=True))
        den = jnp.maximum(jnp.sum(e, axis=0, keepdims=True), 1e-30)
        pn = e * jnp.where(has_cmp, 1.0 / den, 0.0)
        ph = ph + pn
        p_heads.append(pn.astype(BF16))
    o_cmp = _dot_ta(vc_ref[0, 0], jnp.concatenate(p_heads, axis=1))

    imp = ph[0:n_slc, :]
    for r in range(1, CMP_PER_SLC):
        imp = imp + ph[r * n_slc:(r + 1) * n_slc, :]
    blk = lax.broadcasted_iota(I32, (n_slc, Q_BLOCK), 0)
    qp = lane_qpos(n_slc)
    cur = qp // SLC_BLOCK
    forced = (blk == 0) | (blk == cur) | (blk == cur - 1)
    imp = jnp.where(blk * SLC_BLOCK <= qp, imp + jnp.where(forced, FORCE_BONUS, 0.0), NEG_INF)
    key_ref[0] = _sortable_key(imp)
    thr = _kth_largest(key_ref, 1, ksel, n_slc)
    need = _topk_need(key_ref, 1, thr, ksel, n_slc)
    sel, _ = _topk_select(key_ref[0], thr, need, jnp.zeros((1, Q_BLOCK), F32), slt_ref[...])
    sb_ref[...] = jnp.where(sel, 0.0, NEG_INF)

    nkv = (t0 + Q_BLOCK + tk - 1) // tk
    blocks_per_tile = tk // SLC_BLOCK

    def sel_tile(j, carry, diagonal):
        r0 = pl.multiple_of(j * tk, tk)
        bias = jnp.concatenate(
            [jnp.broadcast_to(sb_ref[pl.ds(j * blocks_per_tile + u, 1), :], (SLC_BLOCK, Q_BLOCK))
             for u in range(blocks_per_tile)], axis=0)
        if diagonal:
            kpos = j * tk + lax.broadcasted_iota(I32, (tk, Q_BLOCK), 0)
            bias = jnp.where(kpos <= lane_qpos(tk), bias, NEG_INF)
        s = _dot_tb(ks_ref[0, pl.ds(r0, tk), :], qs)
        return _softmax_tile_t(s, bias, vs_ref[0, pl.ds(r0, tk), :], carry, nh)

    carry = lax.fori_loop(0, nkv - 1, lambda j, c: sel_tile(j, c, False), _softmax_init_t(nh))
    _, l_s, acc_s = sel_tile(nkv - 1, carry, True)
    o_sel = acc_s / jnp.maximum(l_s, 1e-30)

    span = WINDOW + Q_BLOCK
    start = pl.multiple_of(jnp.maximum(t0 - WINDOW, 0), Q_BLOCK)
    dist = lane_qpos(span) - (start + lax.broadcasted_iota(I32, (span, Q_BLOCK), 0))
    bias_w = jnp.where((dist >= 0) & (dist < WINDOW), 0.0, NEG_INF)
    s_w = _dot_tb(kw_ref[0, pl.ds(start, span), :], qs)
    _, l_w, acc_w = _softmax_tile_t(s_w, bias_w, vw_ref[0, pl.ds(start, span), :],
                                    _softmax_init_t(nh), nh)
    o_win = acc_w / jnp.maximum(l_w, 1e-30)

    for h in range(nh):
        sl = slice(h * Q_BLOCK, (h + 1) * Q_BLOCK)
        row = 3 * (g * nh + h)
        o = (gt_ref[pl.ds(row, 1), :] * o_cmp[:, sl]
             + gt_ref[pl.ds(row + 1, 1), :] * o_sel[:, sl]
             + gt_ref[pl.ds(row + 2, 1), :] * o_win[:, sl])
        o_ref[0, :, h * HEAD_DIM:(h + 1) * HEAD_DIM] = o.T.astype(o_ref.dtype)


def nsa_attention(rp3, projb3, proj3, kc, vc, gate_bias_row, slt, *, tk=512):
    B, T, _ = proj3.shape
    tk = min(tk, T)
    n_slc = T // SLC_BLOCK
    ksel = min(SLC_TOPK, n_slc)
    gw = C_GROUP_HEADS * HEAD_DIM
    assert T >= WINDOW + Q_BLOCK and T % tk == 0 and tk % SLC_BLOCK == 0

    def seq(col):
        return pl.BlockSpec((1, T, HEAD_DIM), lambda b, g, i, col=col: (b, 0, col + g))

    cmp_spec = pl.BlockSpec((1, 1, CMP_PER_SLC * n_slc, HEAD_DIM), lambda b, g, i: (b, g, 0, 0))
    return pl.pallas_call(
        functools.partial(_nsa_kernel, tk=tk, n_slc=n_slc, ksel=ksel),
        grid=(B, C_GROUPS, T // Q_BLOCK),
        in_specs=[pl.BlockSpec((1, Q_BLOCK, gw), lambda b, g, i: (b, i, g)),
                  cmp_spec, cmp_spec,
                  seq(PB["c_ks"]), seq(PB["c_vs"]), seq(PB["c_kw"]), seq(PB["c_vw"]),
                  pl.BlockSpec((1, Q_BLOCK, LANES), lambda b, g, i: (b, i, PB["c_g"])),
                  pl.BlockSpec((1, LANES), lambda b, g, i: (0, 0)),
                  pl.BlockSpec((n_slc, n_slc), lambda b, g, i: (0, 0))],
        out_specs=pl.BlockSpec((1, Q_BLOCK, gw), lambda b, g, i: (b, i, g)),
        out_shape=jax.ShapeDtypeStruct((B, T, C_W), BF16),
        scratch_shapes=[pltpu.VMEM((C_GROUP_HEADS * Q_BLOCK, HEAD_DIM), BF16),
                        pltpu.VMEM((LANES, Q_BLOCK), F32),
                        pltpu.VMEM((1, n_slc, Q_BLOCK), I32),
                        pltpu.VMEM((n_slc, Q_BLOCK), F32)],
        compiler_params=_cparams(("parallel", "parallel", "arbitrary")),
        name="nsa_attention",
    )(rp3, kc, vc, rp3, projb3, rp3, projb3, proj3, gate_bias_row, slt)


def _pad_cols(w, width):
    return jnp.pad(w, [(0, 0)] * (w.ndim - 1) + [(0, width - w.shape[-1])])


def _permute_w_in_kernel(w_ref, o_ref):
    offs = np.concatenate([[0], np.cumsum(IN_WIDTHS)])
    src = {n: (int(offs[k]), int(offs[k + 1])) for k, n in enumerate(IN_NAMES)}
    o_ref[...] = jnp.zeros_like(o_ref)
    for name, blk in PB.items():
        a, b = src["b_ik"] if name in ("ik_lo", "ik_hi") else src[name]
        d0 = blk * LANES + (IDX_DIM if name == "ik_hi" else 0)
        fold = QK_FOLD if name in ("a_q", "c_q") else 1.0
        o_ref[:, d0:d0 + (b - a)] = (w_ref[:, a:b] * fold).astype(o_ref.dtype)


def permute_w_in(w_in, *, tk=256):
    depth, D, W = w_in.shape
    tk = min(tk, D)
    return pl.pallas_call(
        _permute_w_in_kernel,
        grid=(depth, D // tk),
        in_specs=[pl.BlockSpec((None, tk, W), lambda l, i: (l, i, 0))],
        out_specs=pl.BlockSpec((None, tk, P_BLOCKS * LANES), lambda l, i: (l, i, 0)),
        out_shape=jax.ShapeDtypeStruct((depth, D, P_BLOCKS * LANES), BF16),
        compiler_params=_cparams(("parallel", "parallel")),
        name="permute_w_in",
    )(w_in)


def _lane_row(v):
    return _pad_cols(v.reshape(1, -1).astype(F32), LANES)


def kernel(x, positions, g_mix_pre, g_mix_post, g_ffn_pre, g_ffn_post, w_in, fox_forget_bias,
           dsa_q_norm, w_dsa_uq, w_dsa_iq, nsa_cmp_pos_k, nsa_cmp_pos_v, w_nsa_cmp_k,
           w_nsa_cmp_v, nsa_gate_bias, w_gate_up, b_gate, w_branch, w_out, w_ffn_gate,
           w_ffn_up, w_ffn_down):
    B, T, D = x.shape
    depth = w_in.shape[0]
    M = B * T
    topk_b = min(IDX_TOPK_MAX, T // 4)
    n_slc = T // SLC_BLOCK
    tk = min(512, T)

    tab128, half128 = rope_tables(positions, HEAD_DIM)
    tab64, half64 = rope_tables(positions, IDX_DIM)
    tables = (tab128, tab64)
    proj_plan = [(0, half128)] * PB["ik_lo"] + [(1, half64)] * (ROPE_BLOCKS - PB["ik_lo"])
    q_plan = [(1, half64)] * QB["q"] + [(0, half128)] * B_HEADS
    slt_tk = strict_lower(tk)
    slt_blk = strict_lower(n_slc)
    d_ff = w_ffn_gate.shape[-1]
    ff_tn = next(t for t in (512, 256, 128) if d_ff % t == 0)
    ff_tk = d_ff // 2 if (d_ff // 2) % LANES == 0 else d_ff

    w_in_b = permute_w_in(w_in)
    w_q = jnp.concatenate([w_dsa_iq, w_dsa_uq * QK_FOLD], axis=-1)
    w_ck = w_nsa_cmp_k.reshape(depth, CMP_BLOCK, HEAD_DIM, HEAD_DIM)
    w_cv = w_nsa_cmp_v.reshape(depth, CMP_BLOCK, HEAD_DIM, HEAD_DIM)
    b_gate3 = b_gate.reshape(depth, 1, -1)

    xf = x.reshape(M, D)
    h = rmsnorm(xf, g_mix_pre[0], width=D)
    for l in range(depth):
        proj, projb = matmul(h, w_in_b, l, (F32, BF16))
        proj3 = proj.reshape(B, T, -1)
        projb3 = projb.reshape(B, T, -1)
        rp3 = rope(proj, proj_plan, tables).reshape(B, T, -1)

        f, ft = fox_gate(proj3, _lane_row(fox_forget_bias[l]))
        o_a = fox_attention(projb3, f, ft)

        cqn = rmsnorm(proj, dsa_q_norm[l], width=B_Q_LORA, col_block=PB["b_cq"] * LANES // B_Q_LORA)
        (qiq,) = matmul(cqn, w_q, l, (F32,), tn=w_q.shape[2] // 2)
        rqiq3 = rope(qiq, q_plan, tables).reshape(B, T, -1)
        o_b = dsa_attention(rqiq3, proj3, rp3, projb3, slt_tk, topk=topk_b, tk=tk)

        kc = nsa_compress(proj3, PB["c_kc"], nsa_cmp_pos_k, w_ck, l)
        vc = nsa_compress(proj3, PB["c_vc"], nsa_cmp_pos_v, w_cv, l)
        o_c = nsa_attention(rp3, projb3, proj3, kc, vc, _lane_row(nsa_gate_bias[l]),
                            slt_blk, tk=min(1024, T))

        y = merge(projb, o_a.reshape(M, -1), o_b.reshape(M, -1), o_c.reshape(M, -1),
                  w_gate_up, b_gate3, w_branch, l, tn=256)
        (mix,) = matmul(y, w_out, l, (F32,))
        xf, h = resnorm(xf, mix, g_mix_post[l], g_ffn_pre[l])

        u = ffn_up(h, w_ffn_gate, w_ffn_up, l, tn=ff_tn)
        ffn = matmul_ksplit(u, w_ffn_down, l, F32, tm=1024, tn=256, tk=ff_tk)
        if l + 1 < depth:
            xf, h = resnorm(xf, ffn, g_ffn_post[l], g_mix_pre[l + 1])
        else:
            xf = resnorm(xf, ffn, g_ffn_post[l])
    return xf.reshape(B, T, D)
```

```python
import functools

import jax
import jax.numpy as jnp
import numpy as np
from jax import lax
from jax.experimental import pallas as pl
from jax.experimental.pallas import tpu as pltpu

F32 = jnp.float32
BF16 = jnp.bfloat16
I32 = jnp.int32

HEAD_DIM = 128
LANES = 128
ROPE_THETA = 500000.0
ROT_FRACTION = 4
NORM_EPS = 1e-6
NEG_INF = -1e30
INT_MIN = -(2 ** 31)

A_HEADS = 12
A_W = A_HEADS * HEAD_DIM
B_HEADS = 8
B_W = B_HEADS * HEAD_DIM
B_Q_LORA = 1024
IDX_HEADS = 32
IDX_DIM = 64
IDX_TOPK_MAX = 256
IDX_SCALE = (IDX_HEADS * IDX_DIM) ** -0.5
C_HEADS = 12
C_W = C_HEADS * HEAD_DIM
C_GROUPS = 2
C_GROUP_HEADS = C_HEADS // C_GROUPS
C_KV_W = C_GROUPS * HEAD_DIM
CMP_BLOCK = 32
CMP_STRIDE = 16
SLC_BLOCK = 64
SLC_TOPK = 16
CMP_PER_SLC = SLC_BLOCK // CMP_STRIDE
WINDOW = 512
FORCE_BONUS = 1e4
GATE_RANK = 512
Q_BLOCK = 128
LOG2E = 1.4426950408889634
QK_FOLD = HEAD_DIM ** -0.5 * LOG2E

IN_WIDTHS = (A_W, A_W, A_W, A_HEADS,
             B_Q_LORA, HEAD_DIM, HEAD_DIM, IDX_DIM, IDX_HEADS,
             C_W, C_KV_W, C_KV_W, C_KV_W, C_KV_W, C_KV_W, C_KV_W, 3 * C_HEADS,
             GATE_RANK)
IN_NAMES = ("a_q", "a_k", "a_v", "a_f", "b_cq", "b_k", "b_v", "b_ik", "b_iw",
            "c_q", "c_kc", "c_vc", "c_ks", "c_vs", "c_kw", "c_vw", "c_g", "z")

PB = dict(c_q=0, b_k=12, c_ks=13, c_kw=15, ik_lo=17, ik_hi=18, b_v=19, z=20, b_cq=24,
          a_q=32, a_k=44, a_v=56, c_kc=68, c_vc=70, c_vs=72, c_vw=74,
          a_f=76, b_iw=77, c_g=78)
P_BLOCKS = 80
ROPE_BLOCKS = 19
QB = dict(iq=0, q=IDX_HEADS * IDX_DIM // LANES)

VMEM_LIMIT = 56 * 1024 * 1024


def _cparams(sem, vmem=VMEM_LIMIT):
    return pltpu.CompilerParams(dimension_semantics=sem, vmem_limit_bytes=vmem)


def _dot(a, b):
    return jnp.dot(a, b, preferred_element_type=F32)


def _dot_tb(a, b):
    return lax.dot_general(a, b, (((1,), (1,)), ((), ())), preferred_element_type=F32)


def _dot_ta(a, b):
    return lax.dot_general(a, b, (((0,), (0,)), ((), ())), preferred_element_type=F32)


def _rmsnorm_kernel(x_ref, g_ref, o_ref):
    x = x_ref[...].astype(F32)
    y = x * lax.rsqrt(jnp.mean(x * x, axis=-1, keepdims=True) + NORM_EPS)
    o_ref[...] = (y * g_ref[...]).astype(o_ref.dtype)


def rmsnorm(x, g, *, width, col_block=0, tm=512):
    M = x.shape[0]
    tm = min(tm, M)
    return pl.pallas_call(
        _rmsnorm_kernel,
        grid=(M // tm,),
        in_specs=[pl.BlockSpec((tm, width), lambda i: (i, col_block)),
                  pl.BlockSpec((1, width), lambda i: (0, 0))],
        out_specs=pl.BlockSpec((tm, width), lambda i: (i, 0)),
        out_shape=jax.ShapeDtypeStruct((M, width), BF16),
        compiler_params=_cparams(("parallel",)),
        name="rmsnorm",
    )(x, g.reshape(1, width).astype(F32))


def _resnorm_kernel(x_ref, m_ref, g1_ref, g2_ref, xo_ref, ho_ref):
    m = m_ref[...]
    y = m * lax.rsqrt(jnp.mean(m * m, axis=-1, keepdims=True) + NORM_EPS) * g1_ref[...]
    xn = x_ref[...] + y
    xo_ref[...] = xn
    h = xn * lax.rsqrt(jnp.mean(xn * xn, axis=-1, keepdims=True) + NORM_EPS) * g2_ref[...]
    ho_ref[...] = h.astype(ho_ref.dtype)


def _resnorm_last_kernel(x_ref, m_ref, g1_ref, xo_ref):
    m = m_ref[...]
    y = m * lax.rsqrt(jnp.mean(m * m, axis=-1, keepdims=True) + NORM_EPS) * g1_ref[...]
    xo_ref[...] = x_ref[...] + y


def resnorm(x, m, g1, g2=None, *, tm=256):
    M, D = x.shape
    tm = min(tm, M)
    row = pl.BlockSpec((tm, D), lambda i: (i, 0))
    gain = pl.BlockSpec((1, D), lambda i: (0, 0))
    if g2 is None:
        return pl.pallas_call(
            _resnorm_last_kernel, grid=(M // tm,),
            in_specs=[row, row, gain], out_specs=row,
            out_shape=jax.ShapeDtypeStruct((M, D), F32),
            compiler_params=_cparams(("parallel",)), name="resnorm_last",
        )(x, m, g1.reshape(1, D))
    return pl.pallas_call(
        _resnorm_kernel, grid=(M // tm,),
        in_specs=[row, row, gain, gain], out_specs=[row, row],
        out_shape=[jax.ShapeDtypeStruct((M, D), F32), jax.ShapeDtypeStruct((M, D), BF16)],
        compiler_params=_cparams(("parallel",)), name="resnorm",
    )(x, m, g1.reshape(1, D), g2.reshape(1, D))


def _mm_kernel(x_ref, w_ref, *o_refs):
    acc = _dot(x_ref[...], w_ref[...])
    for o_ref in o_refs:
        o_ref[...] = acc.astype(o_ref.dtype)


def _layer_cols(layer, K, tn):
    return pl.BlockSpec((None, K, tn), lambda i, j: (layer, 0, j))


def matmul(x, w, layer, out_dtypes, *, tm=1024, tn=512):
    M, K = x.shape
    N = w.shape[2]
    tm, tn = min(tm, M), min(tn, N)
    outs = pl.pallas_call(
        _mm_kernel,
        grid=(M // tm, N // tn),
        in_specs=[pl.BlockSpec((tm, K), lambda i, j: (i, 0)), _layer_cols(layer, K, tn)],
        out_specs=[pl.BlockSpec((tm, tn), lambda i, j: (i, j)) for _ in out_dtypes],
        out_shape=[jax.ShapeDtypeStruct((M, N), dt) for dt in out_dtypes],
        compiler_params=_cparams(("parallel", "arbitrary")),
        name="matmul",
    )(x, w)
    return outs


def _mm_ksplit_kernel(x_ref, w_ref, o_ref, acc_ref):
    k = pl.program_id(2)

    @pl.when(k == 0)
    def _():
        acc_ref[...] = jnp.zeros_like(acc_ref)

    acc_ref[...] += _dot(x_ref[...], w_ref[...])

    @pl.when(k == pl.num_programs(2) - 1)
    def _():
        o_ref[...] = acc_ref[...].astype(o_ref.dtype)


def matmul_ksplit(x, w, layer, out_dtype, *, tm, tn, tk):
    M, K = x.shape
    N = w.shape[2]
    tm, tn, tk = min(tm, M), min(tn, N), min(tk, K)
    return pl.pallas_call(
        _mm_ksplit_kernel,
        grid=(M // tm, N // tn, K // tk),
        in_specs=[pl.BlockSpec((tm, tk), lambda i, j, k: (i, k)),
                  pl.BlockSpec((None, tk, tn), lambda i, j, k: (layer, k, j))],
        out_specs=pl.BlockSpec((tm, tn), lambda i, j, k: (i, j)),
        out_shape=jax.ShapeDtypeStruct((M, N), out_dtype),
        scratch_shapes=[pltpu.VMEM((tm, tn), F32)],
        compiler_params=_cparams(("parallel", "arbitrary", "arbitrary")),
        name="matmul_ksplit",
    )(x, w)


def _ffn_up_kernel(h_ref, wg_ref, wu_ref, o_ref):
    h = h_ref[...]
    g = _dot(h, wg_ref[...].astype(BF16))
    u = _dot(h, wu_ref[...].astype(BF16))
    o_ref[...] = (g * jax.nn.sigmoid(g) * u).astype(o_ref.dtype)


def ffn_up(h, wg, wu, layer, *, tm=1024, tn=256):
    M, K = h.shape
    N = wg.shape[2]
    tm, tn = min(tm, M), min(tn, N)
    return pl.pallas_call(
        _ffn_up_kernel,
        grid=(M // tm, N // tn),
        in_specs=[pl.BlockSpec((tm, K), lambda i, j: (i, 0)),
                  _layer_cols(layer, K, tn), _layer_cols(layer, K, tn)],
        out_specs=pl.BlockSpec((tm, tn), lambda i, j: (i, j)),
        out_shape=jax.ShapeDtypeStruct((M, N), BF16),
        compiler_params=_cparams(("parallel", "arbitrary")),
        name="ffn_up",
    )(h, wg, wu)


def _merge_kernel(z_ref, oa_ref, ob_ref, oc_ref, wga_ref, wgb_ref, wgc_ref,
                  bga_ref, bgb_ref, bgc_ref, wbr_ref, y_ref):
    z = z_ref[...]

    def branch(o_ref, row0, wg_ref, bg_ref):
        gate = jax.nn.sigmoid(_dot(z, wg_ref[...]) + bg_ref[...])
        rows = o_ref.shape[1]
        return gate * _dot(o_ref[...], wbr_ref[row0:row0 + rows, :])

    y = (branch(oa_ref, 0, wga_ref, bga_ref)
         + branch(ob_ref, A_W, wgb_ref, bgb_ref)
         + branch(oc_ref, A_W + B_W, wgc_ref, bgc_ref))
    y_ref[...] = y.astype(y_ref.dtype)


def merge(projb, o_a, o_b, o_c, w_gate_up, b_gate, w_branch, layer, *, tm=1024, tn=512):
    M = projb.shape[0]
    KB, D = w_branch.shape[1:]
    tm, tn = min(tm, M), min(tn, D)
    nj = D // tn
    z_blk = PB["z"] * LANES // GATE_RANK

    def full_k(o):
        return pl.BlockSpec((tm, o.shape[1]), lambda i, j: (i, 0))

    def gate_w(b):
        return pl.BlockSpec((None, GATE_RANK, tn), lambda i, j, b=b: (layer, 0, b * nj + j))

    def gate_b(b):
        return pl.BlockSpec((None, 1, tn), lambda i, j, b=b: (layer, 0, b * nj + j))

    return pl.pallas_call(
        _merge_kernel,
        grid=(M // tm, nj),
        in_specs=[pl.BlockSpec((tm, GATE_RANK), lambda i, j: (i, z_blk)),
                  full_k(o_a), full_k(o_b), full_k(o_c),
                  gate_w(0), gate_w(1), gate_w(2), gate_b(0), gate_b(1), gate_b(2),
                  _layer_cols(layer, KB, tn)],
        out_specs=pl.BlockSpec((tm, tn), lambda i, j: (i, j)),
        out_shape=jax.ShapeDtypeStruct((M, D), BF16),
        compiler_params=_cparams(("parallel", "arbitrary")),
        name="merge",
    )(projb, o_a, o_b, o_c, w_gate_up, w_gate_up, w_gate_up, b_gate, b_gate, b_gate, w_branch)


def _rope_kernel(x_ref, *refs, plan):
    o_ref = refs[-1]
    tables = refs[:-1]
    for c, (ts, half) in enumerate(plan):
        sl = slice(c * LANES, (c + 1) * LANES)
        x = x_ref[:, sl]
        cos, sin_up, sin_dn = (tables[3 * ts + t][...] for t in range(3))
        up = pltpu.roll(x, LANES - half, axis=1)
        dn = pltpu.roll(x, half, axis=1)
        o_ref[:, sl] = (x * cos + up * sin_up + dn * sin_dn).astype(o_ref.dtype)


def rope(x, plan, table_sets, *, tm=512):
    M = x.shape[0]
    tm = min(tm, M)
    width = len(plan) * LANES
    tab = pl.BlockSpec((tm, LANES), lambda i: (i, 0))
    flat = [t for ts in table_sets for t in ts]
    return pl.pallas_call(
        functools.partial(_rope_kernel, plan=tuple(plan)),
        grid=(M // tm,),
        in_specs=[pl.BlockSpec((tm, width), lambda i: (i, 0))] + [tab] * len(flat),
        out_specs=pl.BlockSpec((tm, width), lambda i: (i, 0)),
        out_shape=jax.ShapeDtypeStruct((M, width), BF16),
        compiler_params=_cparams(("parallel",)),
        name="rope",
    )(x, *flat)


def rope_tables(positions, head_dim):
    B, T = positions.shape
    rot = head_dim // ROT_FRACTION
    half = rot // 2
    inv_freq = ROPE_THETA ** (-jnp.arange(half, dtype=F32) / half)
    ang = positions.astype(F32)[:, :, None] * inv_freq
    cos, sin = jnp.cos(ang), jnp.sin(ang)
    rest = head_dim - rot
    zh = jnp.zeros((B, T, half), F32)
    c = jnp.concatenate([cos, cos, jnp.ones((B, T, rest), F32)], axis=-1)
    s1 = jnp.concatenate([-sin, zh, jnp.zeros((B, T, rest), F32)], axis=-1)
    s2 = jnp.concatenate([zh, sin, jnp.zeros((B, T, rest), F32)], axis=-1)
    reps = LANES // head_dim
    return tuple(jnp.tile(t, (1, 1, reps)).reshape(B * T, LANES) for t in (c, s1, s2)), half


def _fox_gate_kernel(a_ref, b_ref, f_ref, ft_ref, carry_ref, *, tb):
    @pl.when(pl.program_id(1) == 0)
    def _():
        carry_ref[...] = jnp.zeros_like(carry_ref)

    x = a_ref[0] + b_ref[...]
    log_f = jnp.minimum(x, 0.0) - jnp.log1p(jnp.exp(-jnp.abs(x)))
    row = lax.broadcasted_iota(I32, (tb, tb), 0)
    col = lax.broadcasted_iota(I32, (tb, tb), 1)
    tri = (col <= row).astype(F32)
    cum = jnp.dot(tri, log_f, preferred_element_type=F32,
                  precision=lax.Precision.HIGHEST) + carry_ref[...]
    cum2 = cum * LOG2E
    f_ref[0] = cum2
    ft_ref[0] = cum2.T[:16, :]
    carry_ref[...] = cum[tb - 1:tb, :]


def fox_gate(proj3, bias_row, *, tb=256):
    B, T, _ = proj3.shape
    tb = min(tb, T)
    return pl.pallas_call(
        functools.partial(_fox_gate_kernel, tb=tb),
        grid=(B, T // tb),
        in_specs=[pl.BlockSpec((1, tb, LANES), lambda b, j: (b, j, PB["a_f"])),
                  pl.BlockSpec((1, LANES), lambda b, j: (0, 0))],
        out_specs=[pl.BlockSpec((1, tb, LANES), lambda b, j: (b, j, 0)),
                   pl.BlockSpec((1, 16, tb), lambda b, j: (b, 0, j))],
        out_shape=[jax.ShapeDtypeStruct((B, T, LANES), F32),
                   jax.ShapeDtypeStruct((B, 16, T), F32)],
        scratch_shapes=[pltpu.VMEM((1, LANES), F32)],
        compiler_params=_cparams(("parallel", "arbitrary")),
        name="fox_gate",
    )(proj3, bias_row)


def _fox_kernel(q_ref, k_ref, v_ref, f_ref, ft_ref, o_ref, *, tq, tk):
    h = pl.program_id(1)
    i = pl.program_id(2)
    q = q_ref[0]
    lane = lax.broadcasted_iota(I32, (tq, LANES), 1)
    fq = jnp.sum(jnp.where(lane == h, f_ref[0], 0.0), axis=1, keepdims=True)
    nkv = (i * tq + tq + tk - 1) // tk

    def update(j, carry, diagonal):
        m, l, acc = carry
        r0 = pl.multiple_of(j * tk, tk)
        fk = ft_ref[0, 0, pl.ds(j, 1), :]
        t = _dot_tb(q, k_ref[0, pl.ds(r0, tk), :]) - fk
        if diagonal:
            qpos = i * tq + lax.broadcasted_iota(I32, (tq, tk), 0)
            kpos = j * tk + lax.broadcasted_iota(I32, (tq, tk), 1)
            t = jnp.where(kpos <= qpos, t, NEG_INF)
        m_new = jnp.maximum(m, jnp.max(t, axis=1, keepdims=True) + fq)
        p = jnp.exp2(t + (fq - m_new))
        alpha = jnp.exp2(m - m_new)
        l = alpha * l + jnp.sum(p, axis=1, keepdims=True)
        acc = alpha * acc + _dot(p.astype(BF16), v_ref[0, pl.ds(r0, tk), :])
        return m_new, l, acc

    init = (jnp.full((tq, 1), NEG_INF, F32), jnp.zeros((tq, 1), F32),
            jnp.zeros((tq, HEAD_DIM), F32))
    carry = lax.fori_loop(0, nkv - 1, lambda j, c: update(j, c, False), init)
    _, l, acc = update(nkv - 1, carry, True)
    o_ref[0] = (acc / jnp.maximum(l, 1e-30)).astype(o_ref.dtype)


def fox_attention(projb3, f, ft, *, tq=1024, tk=1024):
    B, T, _ = projb3.shape
    tq, tk = min(tq, T), min(tk, T)
    assert tk % tq == 0
    ft4 = ft.reshape(B, 16, T // tk, tk)
    return pl.pallas_call(
        functools.partial(_fox_kernel, tq=tq, tk=tk),
        grid=(B, A_HEADS, T // tq),
        in_specs=[pl.BlockSpec((1, tq, HEAD_DIM), lambda b, h, i: (b, i, PB["a_q"] + h)),
                  pl.BlockSpec((1, T, HEAD_DIM), lambda b, h, i: (b, 0, PB["a_k"] + h)),
                  pl.BlockSpec((1, T, HEAD_DIM), lambda b, h, i: (b, 0, PB["a_v"] + h)),
                  pl.BlockSpec((1, tq, LANES), lambda b, h, i: (b, i, 0)),
                  pl.BlockSpec((1, 1, T // tk, tk), lambda b, h, i: (b, h, 0, 0))],
        out_specs=pl.BlockSpec((1, tq, HEAD_DIM), lambda b, h, i: (b, i, h)),
        out_shape=jax.ShapeDtypeStruct((B, T, A_W), BF16),
        compiler_params=_cparams(("parallel", "parallel", "arbitrary")),
        name="fox_attention",
    )(projb3, projb3, projb3, f, ft4)


def _sortable_key(x):
    b = pltpu.bitcast(x, I32)
    return jnp.where(b >= 0, b, b ^ jnp.int32(0x7FFFFFFF))


def _count_ge(key_ref, ntiles, cand, rows):
    def tile(j, cnt):
        ge = jnp.where(key_ref[j] >= cand, 1, 0).astype(I32)
        return cnt + jnp.sum(ge.reshape(rows // 8, 8, LANES), axis=0)

    cnt = lax.fori_loop(0, ntiles, tile, jnp.zeros((8, LANES), I32))
    return jnp.sum(cnt, axis=0, keepdims=True)


def _kth_largest(key_ref, ntiles, k, rows):
    nonneg = _count_ge(key_ref, ntiles, jnp.zeros((1, LANES), I32), rows)
    base0 = jnp.where(nonneg >= k, 0, INT_MIN).astype(I32)

    def step(s, base):
        cand = base | lax.shift_left(jnp.int32(1), 30 - s)
        cnt = _count_ge(key_ref, ntiles, cand, rows)
        return jnp.where(cnt >= k, cand, base)

    return lax.fori_loop(0, 31, step, base0)


def _topk_need(key_ref, ntiles, thr, k, rows):
    gt = _count_ge(key_ref, ntiles, thr + 1, rows)
    gt = jnp.where(thr == jnp.int32(2 ** 31 - 1), 0, gt)
    return (k - gt).astype(F32)


def _topk_select(key, thr, need, eq_before, slt):
    eq = key == thr
    eq_f = jnp.where(eq, 1.0, 0.0)
    rank = _dot(slt, eq_f.astype(BF16)) + eq_before
    sel = (key > thr) | (eq & (rank < need))
    return sel, eq_before + jnp.sum(eq_f, axis=0, keepdims=True)


def strict_lower(n):
    r = jnp.arange(n)
    return (r[None, :] < r[:, None]).astype(BF16)


def _softmax_tile_t(s, bias, v, carry, nheads):
    m, l, acc = carry
    ms, ls, ps = [], [], []
    for h in range(nheads):
        sl = slice(h * LANES, (h + 1) * LANES)
        sh = s[:, sl] + bias
        mh = jnp.maximum(m[:, sl], jnp.max(sh, axis=0, keepdims=True))
        ph = jnp.exp2(sh - mh)
        ms.append(mh)
        ls.append(jnp.sum(ph, axis=0, keepdims=True))
        ps.append(ph.astype(BF16))
    m_new = jnp.concatenate(ms, axis=1)
    alpha = jnp.exp2(m - m_new)
    l = alpha * l + jnp.concatenate(ls, axis=1)
    acc = alpha * acc + _dot_ta(v, jnp.concatenate(ps, axis=1))
    return m_new, l, acc


def _softmax_init_t(nheads):
    width = nheads * LANES
    return (jnp.full((1, width), NEG_INF, F32), jnp.zeros((1, width), F32),
            jnp.zeros((HEAD_DIM, width), F32))


def _dsa_kernel(iq_ref, iw_ref, q_ref, ika_ref, ikb_ref, k_ref, v_ref, slt_ref, o_ref,
                iqs_ref, qs_ref, wt_ref, key_ref, *, tk, topk):
    i = pl.program_id(1)
    t0 = i * Q_BLOCK
    nkv = (t0 + Q_BLOCK + tk - 1) // tk
    npair = IDX_HEADS // 2
    for p in range(npair):
        iqs_ref[p * Q_BLOCK:(p + 1) * Q_BLOCK, :] = iq_ref[0, :, p * LANES:(p + 1) * LANES]
    for h in range(B_HEADS):
        qs_ref[h * Q_BLOCK:(h + 1) * Q_BLOCK, :] = q_ref[0, :, h * HEAD_DIM:(h + 1) * HEAD_DIM]
    wt_ref[...] = (iw_ref[0] * IDX_SCALE).T
    qpos = t0 + lax.broadcasted_iota(I32, (tk, LANES), 1)
    krow = lax.broadcasted_iota(I32, (tk, LANES), 0)
    chunk = 2

    def score_tile(j, _):
        r0 = pl.multiple_of(j * tk, tk)
        ika = ika_ref[0, pl.ds(r0, tk), :]
        ikb = ikb_ref[0, pl.ds(r0, tk), :]
        acc = jnp.zeros((tk, LANES), F32)
        for c in range(npair // chunk):
            iqc = iqs_ref[c * chunk * Q_BLOCK:(c + 1) * chunk * Q_BLOCK, :]
            ra = _dot_tb(ika, iqc)
            rb = _dot_tb(ikb, iqc)
            for u in range(chunk):
                p = c * chunk + u
                sl = slice(u * LANES, (u + 1) * LANES)
                acc = acc + jnp.maximum(ra[:, sl], 0.0) * wt_ref[2 * p:2 * p + 1, :]
                acc = acc + jnp.maximum(rb[:, sl], 0.0) * wt_ref[2 * p + 1:2 * p + 2, :]
        acc = jnp.where((j * tk + krow) <= qpos, acc, NEG_INF)
        key_ref[j] = _sortable_key(acc)
        return 0

    lax.fori_loop(0, nkv, score_tile, 0)
    thr = _kth_largest(key_ref, nkv, topk, tk)
    need = _topk_need(key_ref, nkv, thr, topk, tk)
    qs = qs_ref[...]
    slt = slt_ref[...]

    def attn_tile(j, carry, diagonal):
        eq_before = carry[3]
        r0 = pl.multiple_of(j * tk, tk)
        sel, eq_before = _topk_select(key_ref[j], thr, need, eq_before, slt)
        if diagonal:
            sel = sel & ((j * tk + krow) <= qpos)
        bias = jnp.where(sel, 0.0, NEG_INF)
        s = _dot_tb(k_ref[0, pl.ds(r0, tk), :], qs)
        return _softmax_tile_t(s, bias, v_ref[0, pl.ds(r0, tk), :], carry[:3], B_HEADS) + (eq_before,)

    init = _softmax_init_t(B_HEADS) + (jnp.zeros((1, LANES), F32),)
    carry = lax.fori_loop(0, nkv - 1, lambda j, c: attn_tile(j, c, False), init)
    _, l, acc, _ = attn_tile(nkv - 1, carry, True)
    out = acc / jnp.maximum(l, 1e-30)
    for h in range(B_HEADS):
        o_ref[0, :, h * HEAD_DIM:(h + 1) * HEAD_DIM] = (
            out[:, h * Q_BLOCK:(h + 1) * Q_BLOCK].T.astype(o_ref.dtype))


def dsa_attention(rqiq3, proj3, rp3, projb3, slt, *, topk, tk=512):
    B, T, _ = proj3.shape
    tk = min(tk, T)
    assert tk >= topk and T % tk == 0
    return pl.pallas_call(
        functools.partial(_dsa_kernel, tk=tk, topk=topk),
        grid=(B, T // Q_BLOCK),
        in_specs=[pl.BlockSpec((1, Q_BLOCK, IDX_HEADS * IDX_DIM), lambda b, i: (b, i, QB["iq"])),
                  pl.BlockSpec((1, Q_BLOCK, LANES), lambda b, i: (b, i, PB["b_iw"])),
                  pl.BlockSpec((1, Q_BLOCK, B_W), lambda b, i: (b, i, QB["q"] * LANES // B_W)),
                  pl.BlockSpec((1, T, LANES), lambda b, i: (b, 0, PB["ik_lo"])),
                  pl.BlockSpec((1, T, LANES), lambda b, i: (b, 0, PB["ik_hi"])),
                  pl.BlockSpec((1, T, HEAD_DIM), lambda b, i: (b, 0, PB["b_k"])),
                  pl.BlockSpec((1, T, HEAD_DIM), lambda b, i: (b, 0, PB["b_v"])),
                  pl.BlockSpec((tk, tk), lambda b, i: (0, 0))],
        out_specs=pl.BlockSpec((1, Q_BLOCK, B_W), lambda b, i: (b, i, 0)),
        out_shape=jax.ShapeDtypeStruct((B, T, B_W), BF16),
        scratch_shapes=[pltpu.VMEM((IDX_HEADS // 2 * Q_BLOCK, LANES), BF16),
                        pltpu.VMEM((B_HEADS * Q_BLOCK, HEAD_DIM), BF16),
                        pltpu.VMEM((LANES, Q_BLOCK), F32),
                        pltpu.VMEM((T // tk, tk, LANES), I32)],
        compiler_params=_cparams(("parallel", "arbitrary")),
        name="dsa_attention",
    )(rqiq3, proj3, rqiq3, rp3, rp3, rp3, projb3, slt)


def _compress_kernel(x_ref, pe_ref, w_ref, o_ref, *, n_slc):
    def window_sum(r, lo, hi):
        acc = jnp.zeros((n_slc, HEAD_DIM), F32)
        for l in range(lo, hi):
            off = (CMP_STRIDE * r + l) % SLC_BLOCK
            rows = x_ref[0, pl.ds(off, n_slc, stride=SLC_BLOCK), :] + pe_ref[l:l + 1, :]
            acc = acc + _dot(rows.astype(BF16), w_ref[l])
        return acc

    for r in range(CMP_PER_SLC):
        inside = min(CMP_BLOCK, SLC_BLOCK - CMP_STRIDE * r)
        acc = window_sum(r, 0, inside)
        if inside < CMP_BLOCK:
            acc = acc + pltpu.roll(window_sum(r, inside, CMP_BLOCK), n_slc - 1, axis=0)
        o_ref[0, 0, r * n_slc:(r + 1) * n_slc, :] = acc.astype(o_ref.dtype)


def nsa_compress(proj3, col_block, pe, w4, layer):
    B, T, _ = proj3.shape
    n_slc = T // SLC_BLOCK
    return pl.pallas_call(
        functools.partial(_compress_kernel, n_slc=n_slc),
        grid=(B, C_GROUPS),
        in_specs=[pl.BlockSpec((1, T, HEAD_DIM), lambda b, g: (b, 0, col_block + g)),
                  pl.BlockSpec((None, CMP_BLOCK, HEAD_DIM), lambda b, g: (layer, 0, 0)),
                  pl.BlockSpec((None, CMP_BLOCK, HEAD_DIM, HEAD_DIM), lambda b, g: (layer, 0, 0, 0))],
        out_specs=pl.BlockSpec((1, 1, CMP_PER_SLC * n_slc, HEAD_DIM), lambda b, g: (b, g, 0, 0)),
        out_shape=jax.ShapeDtypeStruct((B, C_GROUPS, CMP_PER_SLC * n_slc, HEAD_DIM), BF16),
        compiler_params=_cparams(("parallel", "parallel")),
        name="nsa_compress",
    )(proj3, pe, w4)


def _nsa_kernel(q_ref, kc_ref, vc_ref, ks_ref, vs_ref, kw_ref, vw_ref, g_ref, gb_ref,
                slt_ref, o_ref, qs_ref, gt_ref, key_ref, sb_ref, *, tk, n_slc, ksel):
    g = pl.program_id(1)
    i = pl.program_id(2)
    t0 = i * Q_BLOCK
    nh = C_GROUP_HEADS
    for h in range(nh):
        qs_ref[h * Q_BLOCK:(h + 1) * Q_BLOCK, :] = q_ref[0, :, h * HEAD_DIM:(h + 1) * HEAD_DIM]
    qs = qs_ref[...]
    gt_ref[...] = jax.nn.sigmoid(g_ref[0] + gb_ref[...]).T

    def lane_qpos(rows):
        return t0 + lax.broadcasted_iota(I32, (rows, Q_BLOCK), 1)

    nc = CMP_PER_SLC * n_slc
    crow = lax.broadcasted_iota(I32, (nc, Q_BLOCK), 0)
    cend = (crow % n_slc) * SLC_BLOCK + (crow // n_slc) * CMP_STRIDE + (CMP_BLOCK - 1)
    bias_c = jnp.where(cend <= lane_qpos(nc), 0.0, NEG_INF)
    has_cmp = lane_qpos(1) >= CMP_BLOCK - 1
    s_c = _dot_tb(kc_ref[0, 0], qs)
    ph = jnp.zeros((nc, Q_BLOCK), F32)
    p_heads = []
    for h in range(nh):
        sh = s_c[:, h * Q_BLOCK:(h + 1) * Q_BLOCK] + bias_c
        e = jnp.exp2(sh - jnp.max(sh, axis=0, keepdims=True))
        den = jnp.maximum(jnp.sum(e, axis=0, keepdims=True), 1e-30)
        pn = e * jnp.where(has_cmp, 1.0 / den, 0.0)
        ph = ph + pn
        p_heads.append(pn.astype(BF16))
    o_cmp = _dot_ta(vc_ref[0, 0], jnp.concatenate(p_heads, axis=1))

    imp = ph[0:n_slc, :]
    for r in range(1, CMP_PER_SLC):
        imp = imp + ph[r * n_slc:(r + 1) * n_slc, :]
    blk = lax.broadcasted_iota(I32, (n_slc, Q_BLOCK), 0)
    qp = lane_qpos(n_slc)
    cur = qp // SLC_BLOCK
    forced = (blk == 0) | (blk == cur) | (blk == cur - 1)
    imp = jnp.where(blk * SLC_BLOCK <= qp, imp + jnp.where(forced, FORCE_BONUS, 0.0), NEG_INF)
    key_ref[0] = _sortable_key(imp)
    thr = _kth_largest(key_ref, 1, ksel, n_slc)
    need = _topk_need(key_ref, 1, thr, ksel, n_slc)
    sel, _ = _topk_select(key_ref[0], thr, need, jnp.zeros((1, Q_BLOCK), F32), slt_ref[...])
    sb_ref[...] = jnp.where(sel, 0.0, NEG_INF)

    nkv = (t0 + Q_BLOCK + tk - 1) // tk
    blocks_per_tile = tk // SLC_BLOCK

    def sel_tile(j, carry, diagonal):
        r0 = pl.multiple_of(j * tk, tk)
        bias = jnp.concatenate(
            [jnp.broadcast_to(sb_ref[pl.ds(j * blocks_per_tile + u, 1), :], (SLC_BLOCK, Q_BLOCK))
             for u in range(blocks_per_tile)], axis=0)
        if diagonal:
            kpos = j * tk + lax.broadcasted_iota(I32, (tk, Q_BLOCK), 0)
            bias = jnp.where(kpos <= lane_qpos(tk), bias, NEG_INF)
        s = _dot_tb(ks_ref[0, pl.ds(r0, tk), :], qs)
        return _softmax_tile_t(s, bias, vs_ref[0, pl.ds(r0, tk), :], carry, nh)

    carry = lax.fori_loop(0, nkv - 1, lambda j, c: sel_tile(j, c, False), _softmax_init_t(nh))
    _, l_s, acc_s = sel_tile(nkv - 1, carry, True)
    o_sel = acc_s / jnp.maximum(l_s, 1e-30)

    span = WINDOW + Q_BLOCK
    start = pl.multiple_of(jnp.maximum(t0 - WINDOW, 0), Q_BLOCK)
    dist = lane_qpos(span) - (start + lax.broadcasted_iota(I32, (span, Q_BLOCK), 0))
    bias_w = jnp.where((dist >= 0) & (dist < WINDOW), 0.0, NEG_INF)
    s_w = _dot_tb(kw_ref[0, pl.ds(start, span), :], qs)
    _, l_w, acc_w = _softmax_tile_t(s_w, bias_w, vw_ref[0, pl.ds(start, span), :],
                                    _softmax_init_t(nh), nh)
    o_win = acc_w / jnp.maximum(l_w, 1e-30)

    for h in range(nh):
        sl = slice(h * Q_BLOCK, (h + 1) * Q_BLOCK)
        row = 3 * (g * nh + h)
        o = (gt_ref[pl.ds(row, 1), :] * o_cmp[:, sl]
             + gt_ref[pl.ds(row + 1, 1), :] * o_sel[:, sl]
             + gt_ref[pl.ds(row + 2, 1), :] * o_win[:, sl])
        o_ref[0, :, h * HEAD_DIM:(h + 1) * HEAD_DIM] = o.T.astype(o_ref.dtype)


def nsa_attention(rp3, projb3, proj3, kc, vc, gate_bias_row, slt, *, tk=512):
    B, T, _ = proj3.shape
    tk = min(tk, T)
    n_slc = T // SLC_BLOCK
    ksel = min(SLC_TOPK, n_slc)
    gw = C_GROUP_HEADS * HEAD_DIM
    assert T >= WINDOW + Q_BLOCK and T % tk == 0 and tk % SLC_BLOCK == 0

    def seq(col):
        return pl.BlockSpec((1, T, HEAD_DIM), lambda b, g, i, col=col: (b, 0, col + g))

    cmp_spec = pl.BlockSpec((1, 1, CMP_PER_SLC * n_slc, HEAD_DIM), lambda b, g, i: (b, g, 0, 0))
    return pl.pallas_call(
        functools.partial(_nsa_kernel, tk=tk, n_slc=n_slc, ksel=ksel),
        grid=(B, C_GROUPS, T // Q_BLOCK),
        in_specs=[pl.BlockSpec((1, Q_BLOCK, gw), lambda b, g, i: (b, i, g)),
                  cmp_spec, cmp_spec,
                  seq(PB["c_ks"]), seq(PB["c_vs"]), seq(PB["c_kw"]), seq(PB["c_vw"]),
                  pl.BlockSpec((1, Q_BLOCK, LANES), lambda b, g, i: (b, i, PB["c_g"])),
                  pl.BlockSpec((1, LANES), lambda b, g, i: (0, 0)),
                  pl.BlockSpec((n_slc, n_slc), lambda b, g, i: (0, 0))],
        out_specs=pl.BlockSpec((1, Q_BLOCK, gw), lambda b, g, i: (b, i, g)),
        out_shape=jax.ShapeDtypeStruct((B, T, C_W), BF16),
        scratch_shapes=[pltpu.VMEM((C_GROUP_HEADS * Q_BLOCK, HEAD_DIM), BF16),
                        pltpu.VMEM((LANES, Q_BLOCK), F32),
                        pltpu.VMEM((1, n_slc, Q_BLOCK), I32),
                        pltpu.VMEM((n_slc, Q_BLOCK), F32)],
        compiler_params=_cparams(("parallel", "parallel", "arbitrary")),
        name="nsa_attention",
    )(rp3, kc, vc, rp3, projb3, rp3, projb3, proj3, gate_bias_row, slt)


def _pad_cols(w, width):
    return jnp.pad(w, [(0, 0)] * (w.ndim - 1) + [(0, width - w.shape[-1])])


def _permute_w_in_kernel(w_ref, o_ref):
    offs = np.concatenate([[0], np.cumsum(IN_WIDTHS)])
    src = {n: (int(offs[k]), int(offs[k + 1])) for k, n in enumerate(IN_NAMES)}
    o_ref[...] = jnp.zeros_like(o_ref)
    for name, blk in PB.items():
        a, b = src["b_ik"] if name in ("ik_lo", "ik_hi") else src[name]
        d0 = blk * LANES + (IDX_DIM if name == "ik_hi" else 0)
        fold = QK_FOLD if name in ("a_q", "c_q") else 1.0
        o_ref[:, d0:d0 + (b - a)] = (w_ref[:, a:b] * fold).astype(o_ref.dtype)


def permute_w_in(w_in, *, tk=256):
    depth, D, W = w_in.shape
    tk = min(tk, D)
    return pl.pallas_call(
        _permute_w_in_kernel,
        grid=(depth, D // tk),
        in_specs=[pl.BlockSpec((None, tk, W), lambda l, i: (l, i, 0))],
        out_specs=pl.BlockSpec((None, tk, P_BLOCKS * LANES), lambda l, i: (l, i, 0)),
        out_shape=jax.ShapeDtypeStruct((depth, D, P_BLOCKS * LANES), BF16),
        compiler_params=_cparams(("parallel", "parallel")),
        name="permute_w_in",
    )(w_in)


def _lane_row(v):
    return _pad_cols(v.reshape(1, -1).astype(F32), LANES)


def kernel(x, positions, g_mix_pre, g_mix_post, g_ffn_pre, g_ffn_post, w_in, fox_forget_bias,
           dsa_q_norm, w_dsa_uq, w_dsa_iq, nsa_cmp_pos_k, nsa_cmp_pos_v, w_nsa_cmp_k,
           w_nsa_cmp_v, nsa_gate_bias, w_gate_up, b_gate, w_branch, w_out, w_ffn_gate,
           w_ffn_up, w_ffn_down):
    B, T, D = x.shape
    depth = w_in.shape[0]
    M = B * T
    topk_b = min(IDX_TOPK_MAX, T // 4)
    n_slc = T // SLC_BLOCK
    tk = min(512, T)

    tab128, half128 = rope_tables(positions, HEAD_DIM)
    tab64, half64 = rope_tables(positions, IDX_DIM)
    tables = (tab128, tab64)
    proj_plan = [(0, half128)] * PB["ik_lo"] + [(1, half64)] * (ROPE_BLOCKS - PB["ik_lo"])
    q_plan = [(1, half64)] * QB["q"] + [(0, half128)] * B_HEADS
    slt_tk = strict_lower(tk)
    slt_blk = strict_lower(n_slc)
    d_ff = w_ffn_gate.shape[-1]
    ff_tn = next(t for t in (512, 256, 128) if d_ff % t == 0)
    ff_tk = d_ff // 2 if (d_ff // 2) % LANES == 0 else d_ff

    w_in_b = permute_w_in(w_in)
    w_q_b = jnp.concatenate([w_dsa_iq, w_dsa_uq * QK_FOLD], axis=-1).astype(BF16)
    w_ck_b = w_nsa_cmp_k.reshape(depth, CMP_BLOCK, HEAD_DIM, HEAD_DIM).astype(BF16)
    w_cv_b = w_nsa_cmp_v.reshape(depth, CMP_BLOCK, HEAD_DIM, HEAD_DIM).astype(BF16)
    w_gate_b = w_gate_up.astype(BF16)
    b_gate3 = b_gate.reshape(depth, 1, -1)
    w_branch_b = w_branch.astype(BF16)
    w_out_b = w_out.astype(BF16)
    w_fd_b = w_ffn_down.astype(BF16)

    xf = x.reshape(M, D)
    h = rmsnorm(xf, g_mix_pre[0], width=D)
    for l in range(depth):
        proj, projb = matmul(h, w_in_b, l, (F32, BF16))
        proj3 = proj.reshape(B, T, -1)
        projb3 = projb.reshape(B, T, -1)
        rp3 = rope(proj, proj_plan, tables).reshape(B, T, -1)

        f, ft = fox_gate(proj3, _lane_row(fox_forget_bias[l]))
        o_a = fox_attention(projb3, f, ft)

        cqn = rmsnorm(proj, dsa_q_norm[l], width=B_Q_LORA, col_block=PB["b_cq"] * LANES // B_Q_LORA)
        (qiq,) = matmul(cqn, w_q_b, l, (F32,), tn=w_q_b.shape[2] // 2)
        rqiq3 = rope(qiq, q_plan, tables).reshape(B, T, -1)
        o_b = dsa_attention(rqiq3, proj3, rp3, projb3, slt_tk, topk=topk_b, tk=tk)

        kc = nsa_compress(proj3, PB["c_kc"], nsa_cmp_pos_k, w_ck_b, l)
        vc = nsa_compress(proj3, PB["c_vc"], nsa_cmp_pos_v, w_cv_b, l)
        o_c = nsa_attention(rp3, projb3, proj3, kc, vc, _lane_row(nsa_gate_bias[l]),
                            slt_blk, tk=min(1024, T))

        y = merge(projb, o_a.reshape(M, -1), o_b.reshape(M, -1), o_c.reshape(M, -1),
                  w_gate_b, b_gate3, w_branch_b, l)
        (mix,) = matmul(y, w_out_b, l, (F32,))
        xf, h = resnorm(xf, mix, g_mix_post[l], g_ffn_pre[l])

        u = ffn_up(h, w_ffn_gate, w_ffn_up, l, tn=ff_tn)
        ffn = matmul_ksplit(u, w_fd_b, l, F32, tm=1024, tn=512, tk=ff_tk)
        if l + 1 < depth:
            xf, h = resnorm(xf, ffn, g_ffn_post[l], g_mix_pre[l + 1])
        else:
            xf = resnorm(xf, ffn, g_ffn_post[l])
    return xf.reshape(B, T, D)
```
